```python
import jax, jax.numpy as jnp
from jax import lax
import numpy as np

D_MODEL = 2048
BATCH = 4
SEQ = 4096
DEPTH = 2

N_EVEN = (DEPTH + 1) // 2
N_ODD = DEPTH // 2

HEAD_DIM = 128
NSA_HEADS = D_MODEL // (2 * HEAD_DIM)
NSA_KV_HEADS = max(1, NSA_HEADS // 4)
NSA_GROUP = NSA_HEADS // NSA_KV_HEADS
NSA_WIDTH = NSA_HEADS * HEAD_DIM
KV_WIDTH = NSA_KV_HEADS * HEAD_DIM
N_BRANCH = 3
CMP_BLOCK = 32
CMP_STRIDE = 16
SLC_BLOCK = 64
N_SLC = 16
N_LOCAL_SLC = 2
WINDOW = 512
WIN_Q_BLOCK = 128
SEL_Q_BLOCK = 64
FORCE_SCORE = 1e9

GMLP_GROUP_DIM = 128
GMLP_GROUPS = D_MODEL // (2 * GMLP_GROUP_DIM)
GMLP_WIDTH = GMLP_GROUPS * GMLP_GROUP_DIM
GMLP_CHUNK = 128

EVEN_SPLITS = [NSA_WIDTH] + [KV_WIDTH] * 6 + [N_BRANCH * NSA_HEADS, GMLP_WIDTH, GMLP_WIDTH]
EVEN_IN_WIDTH = sum(EVEN_SPLITS)
MIX_OUT_WIDTH = NSA_WIDTH + GMLP_WIDTH

CONV_WIDTH = 3
SCONV_WIDTH = D_MODEL

D_FF = ((8 * D_MODEL // 3) + 255) // 256 * 256

EPS = 1e-6
NEG = -1e30

kernel_name = "hybrid_nsa_gmlp_shortconv_convffn"


def rms_norm(x, g):
    xf = x.astype(jnp.float32)
    y = xf * lax.rsqrt(jnp.mean(xf * xf, axis=-1, keepdims=True) + EPS)
    return (y * g.astype(jnp.float32)).astype(x.dtype)


def layer_norm(x, g):
    xf = x.astype(jnp.float32)
    xc = xf - jnp.mean(xf, axis=-1, keepdims=True)
    y = xc * lax.rsqrt(jnp.mean(xc * xc, axis=-1, keepdims=True) + EPS)
    return (y * g.astype(jnp.float32)).astype(x.dtype)


def causal_dwconv3(x, w):
    s = x.shape[1]
    xp = jnp.pad(x, ((0, 0), (CONV_WIDTH - 1, 0), (0, 0)))
    return w[0] * xp[:, 0:s] + w[1] * xp[:, 1:s + 1] + w[2] * xp[:, 2:s + 2]


def masked_softmax(s, mask):
    s = jnp.where(mask, s.astype(jnp.float32), NEG)
    m = jnp.max(s, axis=-1, keepdims=True)
    p = jnp.exp(s - m) * mask
    return p / jnp.maximum(jnp.sum(p, axis=-1, keepdims=True), 1e-30)


def nsa_mixer(q, k_cmp, v_cmp, k_slc, v_slc, k_win, v_win, gate_logits,
              q_gain, k_gain, cmp_pe, cmp_k_w1, cmp_k_w2, cmp_v_w1, cmp_v_w2):
    b_, s, _, d = q.shape
    g_, r_ = NSA_KV_HEADS, NSA_GROUP
    pos = jnp.arange(s)
    qg = (rms_norm(q, q_gain) * (d ** -0.5)).reshape(b_, s, g_, r_, d)

    n_cmp = (s - CMP_BLOCK) // CMP_STRIDE + 1
    blk_idx = jnp.arange(n_cmp)[:, None] * CMP_STRIDE + jnp.arange(CMP_BLOCK)[None, :]

    def compress(kv, w1, w2):
        blocks = kv[:, blk_idx] + cmp_pe[:, None, :]
        h = jax.nn.gelu(jnp.einsum('bnlgd,lde->bnge', blocks, w1))
        return jnp.einsum('bnge,ef->bngf', h, w2)

    kc = rms_norm(compress(k_cmp, cmp_k_w1, cmp_k_w2), k_gain[0])
    vc = compress(v_cmp, cmp_v_w1, cmp_v_w2)
    cmp_end = jnp.arange(n_cmp) * CMP_STRIDE + CMP_BLOCK - 1
    cmp_mask = cmp_end[None, :] <= pos[:, None]
    p_cmp = masked_softmax(jnp.einsum('bsgrd,bngd->bgrsn', qg, kc), cmp_mask)
    o_cmp = jnp.einsum('bgrsn,bngd->bsgrd', p_cmp.astype(vc.dtype), vc)

    n_slc = s // SLC_BLOCK
    n_top = min(N_SLC, n_slc)
    cs = jnp.arange(n_cmp) * CMP_STRIDE
    ss = jnp.arange(n_slc) * SLC_BLOCK
    overlap = ((cs[:, None] < ss[None, :] + SLC_BLOCK) &
               (cs[:, None] + CMP_BLOCK > ss[None, :])).astype(jnp.float32)
    imp = jnp.einsum('bgrsn,nj->bgsj', p_cmp, overlap)
    jb = jnp.arange(n_slc)
    cur = pos // SLC_BLOCK
    causal_blk = ss[None, :] <= pos[:, None]
    forced = (jb[None, :] == 0) | ((cur[:, None] - jb[None, :] >= 0) &
                                   (cur[:, None] - jb[None, :] < N_LOCAL_SLC))
    imp = jnp.where(causal_blk, imp, NEG)
    imp = jnp.where(forced, FORCE_SCORE, imp)
    _, sel = lax.top_k(imp, n_top)

    kb = rms_norm(k_slc, k_gain[1]).reshape(b_, n_slc, SLC_BLOCK, g_, d).transpose(0, 3, 1, 2, 4)
    vb = v_slc.reshape(b_, n_slc, SLC_BLOCK, g_, d).transpose(0, 3, 1, 2, 4)
    nq = s // SEL_Q_BLOCK
    q_blocks = qg.reshape(b_, nq, SEL_Q_BLOCK, g_, r_, d).transpose(1, 0, 3, 4, 2, 5)
    sel_blocks = sel.reshape(b_, g_, nq, SEL_Q_BLOCK, n_top).transpose(2, 0, 1, 3, 4)
    bi = jnp.arange(b_)[:, None, None, None]
    gi = jnp.arange(g_)[None, :, None, None]
    n_keys = n_top * SLC_BLOCK

    def sel_block(args):
        qb, idx, start = args
        kg = kb[bi, gi, idx].reshape(b_, g_, SEL_Q_BLOCK, n_keys, d)
        vg = vb[bi, gi, idx].reshape(b_, g_, SEL_Q_BLOCK, n_keys, d)
        kpos = (idx[..., None] * SLC_BLOCK + jnp.arange(SLC_BLOCK)).reshape(b_, g_, SEL_Q_BLOCK, n_keys)
        qpos = start + jnp.arange(SEL_Q_BLOCK)
        mask = kpos[:, :, None] <= qpos[None, None, None, :, None]
        p = masked_softmax(jnp.einsum('bgrqd,bgqkd->bgrqk', qb, kg), mask)
        return jnp.einsum('bgrqk,bgqkd->bgrqd', p.astype(vg.dtype), vg)

    o_slc = lax.map(sel_block, (q_blocks, sel_blocks, jnp.arange(nq) * SEL_Q_BLOCK))
    o_slc = o_slc.transpose(1, 0, 4, 2, 3, 5).reshape(b_, s, g_, r_, d)

    nw = s // WIN_Q_BLOCK
    n_prev = WINDOW // WIN_Q_BLOCK
    kband = (n_prev + 1) * WIN_Q_BLOCK
    pad = ((0, 0), (WINDOW, 0), (0, 0), (0, 0))
    kp = jnp.pad(rms_norm(k_win, k_gain[2]), pad).reshape(b_, nw + n_prev, WIN_Q_BLOCK, g_, d)
    vp = jnp.pad(v_win, pad).reshape(b_, nw + n_prev, WIN_Q_BLOCK, g_, d)
    k_band = jnp.concatenate([kp[:, j:j + nw] for j in range(n_prev + 1)], axis=2)
    v_band = jnp.concatenate([vp[:, j:j + nw] for j in range(n_prev + 1)], axis=2)
    q_w = qg.reshape(b_, nw, WIN_Q_BLOCK, g_, r_, d)
    c0 = jnp.arange(nw)[:, None, None] * WIN_Q_BLOCK
    qpos_w = c0 + jnp.arange(WIN_Q_BLOCK)[None, :, None]
    kpos_w = c0 - WINDOW + jnp.arange(kband)[None, None, :]
    win_mask = (kpos_w >= 0) & (kpos_w <= qpos_w) & (qpos_w - kpos_w < WINDOW)
    p_win = masked_softmax(jnp.einsum('bcqgrd,bckgd->bgrcqk', q_w, k_band), win_mask)
    o_win = jnp.einsum('bgrcqk,bckgd->bcqgrd', p_win.astype(v_band.dtype), v_band).reshape(b_, s, g_, r_, d)

    gates = jax.nn.sigmoid(gate_logits).reshape(b_, s, g_, r_, N_BRANCH)
    o = gates[..., 0:1] * o_cmp + gates[..., 1:2] * o_slc + gates[..., 2:3] * o_win
    return o.reshape(b_, s, NSA_WIDTH)


def gmlp_mixer(uv, v_gain, w_s, b_s):
    b_, s, _ = uv.shape
    z = jax.nn.gelu(uv)
    u, v = z[..., :GMLP_WIDTH], z[..., GMLP_WIDTH:]
    v = layer_norm(v.reshape(b_, s, GMLP_GROUPS, GMLP_GROUP_DIM), v_gain)
    nc = s // GMLP_CHUNK
    vc = v.reshape(b_, nc, GMLP_CHUNK, GMLP_GROUPS, GMLP_GROUP_DIM)
    w = w_s * jnp.tril(jnp.ones((GMLP_CHUNK, GMLP_CHUNK), w_s.dtype))
    mixed = jnp.einsum('gts,bcsgd->bctgd', w, vc) + b_s.T[None, None, :, :, None]
    return u * mixed.reshape(b_, s, GMLP_WIDTH)


def even_mixer(x, norm_g, w_in, q_gain, k_gain, cmp_pe, cmp_k_w1, cmp_k_w2, cmp_v_w1, cmp_v_w2,
               gmlp_norm, gmlp_ws, gmlp_b, w_out):
    b_, s, _ = x.shape
    p = rms_norm(x, norm_g) @ w_in
    q, kc, vc, ks, vs, kw, vw, gl, uv_u, uv_v = jnp.split(p, list(np.cumsum(EVEN_SPLITS)[:-1]), axis=-1)
    kv = lambda t: t.reshape(b_, s, NSA_KV_HEADS, HEAD_DIM)
    o_a = nsa_mixer(q.reshape(b_, s, NSA_HEADS, HEAD_DIM), kv(kc), kv(vc), kv(ks), kv(vs), kv(kw), kv(vw),
                    gl.reshape(b_, s, NSA_HEADS, N_BRANCH), q_gain, k_gain, cmp_pe,
                    cmp_k_w1, cmp_k_w2, cmp_v_w1, cmp_v_w2)
    o_b = gmlp_mixer(jnp.concatenate([uv_u, uv_v], axis=-1), gmlp_norm, gmlp_ws, gmlp_b)
    return jnp.concatenate([o_a, o_b], axis=-1) @ w_out


def odd_mixer(x, norm_g, w_in, conv_w, w_out):
    bg, cg, xt = jnp.split(rms_norm(x, norm_g) @ w_in, 3, axis=-1)
    return (bg * causal_dwconv3(cg * xt, conv_w)) @ w_out


def conv_ffn(x, norm_g, w_in, conv_w, conv_b, w_down):
    g, u = jnp.split(rms_norm(x, norm_g) @ w_in, 2, axis=-1)
    g = causal_dwconv3(g, conv_w) + conv_b
    return (jax.nn.silu(g) * u) @ w_down


def setup_inputs(seed: int = 0) -> dict:
    key = jax.random.key(seed)
    ks = iter(jax.random.split(key, 32))

    def nrm(shape, scale):
        return jax.random.normal(next(ks), shape, jnp.float32) * scale

    def gain(shape):
        return 1.0 + nrm(shape, 0.02)

    d = HEAD_DIM
    return {
        "x": nrm((BATCH, SEQ, D_MODEL), 1.0),
        "ev_norm": gain((N_EVEN, D_MODEL)),
        "ev_w_in": nrm((N_EVEN, D_MODEL, EVEN_IN_WIDTH), D_MODEL ** -0.5),
        "ev_q_gain": gain((N_EVEN, d)),
        "ev_k_gain": gain((N_EVEN, N_BRANCH, d)),
        "ev_cmp_pe": nrm((N_EVEN, CMP_BLOCK, d), 0.2),
        "ev_cmp_k_w1": nrm((N_EVEN, CMP_BLOCK, d, d), (CMP_BLOCK * d) ** -0.5),
        "ev_cmp_k_w2": nrm((N_EVEN, d, d), d ** -0.5),
        "ev_cmp_v_w1": nrm((N_EVEN, CMP_BLOCK, d, d), (CMP_BLOCK * d) ** -0.5),
        "ev_cmp_v_w2": nrm((N_EVEN, d, d), d ** -0.5),
        "ev_gmlp_norm": gain((N_EVEN, GMLP_GROUPS, GMLP_GROUP_DIM)),
        "ev_gmlp_ws": nrm((N_EVEN, GMLP_GROUPS, GMLP_CHUNK, GMLP_CHUNK), 0.5 * GMLP_CHUNK ** -0.5),
        "ev_gmlp_b": gain((N_EVEN, GMLP_GROUPS, GMLP_CHUNK)),
        "ev_w_out": nrm((N_EVEN, MIX_OUT_WIDTH, D_MODEL), MIX_OUT_WIDTH ** -0.5),
        "od_norm": gain((N_ODD, D_MODEL)),
        "od_w_in": nrm((N_ODD, D_MODEL, 3 * SCONV_WIDTH), D_MODEL ** -0.5),
        "od_conv_w": nrm((N_ODD, CONV_WIDTH, SCONV_WIDTH), CONV_WIDTH ** -0.5),
        "od_w_out": nrm((N_ODD, SCONV_WIDTH, D_MODEL), SCONV_WIDTH ** -0.5),
        "ffn_norm": gain((DEPTH, D_MODEL)),
        "ffn_w_in": nrm((DEPTH, D_MODEL, 2 * D_FF), D_MODEL ** -0.5),
        "ffn_conv_w": nrm((DEPTH, CONV_WIDTH, D_FF), CONV_WIDTH ** -0.5),
        "ffn_conv_b": nrm((DEPTH, D_FF), 0.01),
        "ffn_w_down": nrm((DEPTH, D_FF, D_MODEL), D_FF ** -0.5),
    }


def reference(x, ev_norm, ev_w_in, ev_q_gain, ev_k_gain, ev_cmp_pe, ev_cmp_k_w1, ev_cmp_k_w2,
              ev_cmp_v_w1, ev_cmp_v_w2, ev_gmlp_norm, ev_gmlp_ws, ev_gmlp_b, ev_w_out,
              od_norm, od_w_in, od_conv_w, od_w_out,
              ffn_norm, ffn_w_in, ffn_conv_w, ffn_conv_b, ffn_w_down):
    for i in range(DEPTH):
        j = i // 2
        if i % 2 == 0:
            x = x + even_mixer(x, ev_norm[j], ev_w_in[j], ev_q_gain[j], ev_k_gain[j], ev_cmp_pe[j],
                               ev_cmp_k_w1[j], ev_cmp_k_w2[j], ev_cmp_v_w1[j], ev_cmp_v_w2[j],
                               ev_gmlp_norm[j], ev_gmlp_ws[j], ev_gmlp_b[j], ev_w_out[j])
        else:
            x = x + odd_mixer(x, od_norm[j], od_w_in[j], od_conv_w[j], od_w_out[j])
        x = x + conv_ffn(x, ffn_norm[i], ffn_w_in[i], ffn_conv_w[i], ffn_conv_b[i], ffn_w_down[i])
    return x
```

```python
import functools

import jax
import jax.numpy as jnp
from jax import lax
from jax.experimental import pallas as pl
from jax.experimental.pallas import tpu as pltpu

F32 = jnp.float32
BF16 = jnp.bfloat16

HEAD_DIM = 128
NSA_GROUP = 4
N_BRANCH = 3
CMP_BLOCK = 32
CMP_STRIDE = 16
SLC_BLOCK = 64
N_SLC = 16
N_LOCAL_SLC = 2
WINDOW = 512
FORCE_SCORE = 1e9
GMLP_GROUP_DIM = 128
GMLP_CHUNK = 128
CONV_WIDTH = 3
EPS = 1e-6
NEG = -1e30

LANES = 128
BF16_SUBLANES = 16
VMEM_LIMIT = 56 * 1024 * 1024

HALO = BF16_SUBLANES


def _params(*sem):
    return pltpu.CompilerParams(dimension_semantics=sem, vmem_limit_bytes=VMEM_LIMIT)


def _norm_matmul_body(x_ref, g_ref, w_ref, o_ref, xn_ref):
    @pl.when(pl.program_id(1) == 0)
    def _():
        x = x_ref[...]
        ms = jnp.mean(x * x, axis=-1, keepdims=True)
        xn_ref[...] = (x * lax.rsqrt(ms + EPS) * g_ref[...]).astype(BF16)

    o_ref[...] = jnp.dot(xn_ref[...], w_ref[...], preferred_element_type=F32).astype(o_ref.dtype)


def norm_matmul(x, g, w, *, tm, tn):
    t, d = x.shape
    n = w.shape[1]
    assert t % tm == 0 and n % tn == 0
    return pl.pallas_call(
        _norm_matmul_body,
        grid=(t // tm, n // tn),
        in_specs=[pl.BlockSpec((tm, d), lambda i, j: (i, 0)),
                  pl.BlockSpec((1, d), lambda i, j: (0, 0)),
                  pl.BlockSpec((d, tn), lambda i, j: (0, j))],
        out_specs=pl.BlockSpec((tm, tn), lambda i, j: (i, j)),
        out_shape=jax.ShapeDtypeStruct((t, n), BF16),
        scratch_shapes=[pltpu.VMEM((tm, d), BF16)],
        compiler_params=_params("parallel", "arbitrary"),
        name="norm_matmul",
    )(x, g.reshape(1, d), w)


def _shift_rows(cur, halo, k):
    out = pltpu.roll(cur, k, axis=0)
    row = lax.broadcasted_iota(jnp.int32, cur.shape, 0)
    for r in range(k):
        src = HALO - k + r
        out = jnp.where(row == r, halo[src:src + 1, :], out)
    return out


def _conv3(cur, halo, cw):
    return (cw[0:1, :] * _shift_rows(cur, halo, 2) + cw[1:2, :] * _shift_rows(cur, halo, 1)
            + cw[2:3, :] * cur)


def _ffn_down_body(x_ref, g_ref, gh_ref, u_ref, cw_ref, cb_ref, w_ref, o_ref, *, seq_tiles):
    @pl.when(pl.program_id(1) == 0)
    def _():
        o_ref[...] = x_ref[...]

    keep = (pl.program_id(0) % seq_tiles != 0).astype(F32)
    g = g_ref[...].astype(F32)
    gh = gh_ref[...].astype(F32) * keep
    conv = _conv3(g, gh, cw_ref[...]) + cb_ref[...]
    h = jax.nn.silu(conv) * u_ref[...].astype(F32)
    o_ref[...] += jnp.dot(h.astype(BF16), w_ref[...], preferred_element_type=F32)


def _sconv_out_body(x_ref, a_ref, c_ref, ch_ref, d_ref, dh_ref, cw_ref, w_ref, o_ref, *, seq_tiles):
    @pl.when(pl.program_id(1) == 0)
    def _():
        o_ref[...] = x_ref[...]

    keep = (pl.program_id(0) % seq_tiles != 0).astype(F32)
    m = c_ref[...].astype(F32) * d_ref[...].astype(F32)
    mh = ch_ref[...].astype(F32) * dh_ref[...].astype(F32) * keep
    h = a_ref[...].astype(F32) * _conv3(m, mh, cw_ref[...])
    o_ref[...] += jnp.dot(h.astype(BF16), w_ref[...], preferred_element_type=F32)


def _halo_spec(tm, tk, col_off):
    per = tm // HALO
    return pl.BlockSpec((HALO, tk), lambda i, k: (jnp.maximum(i * per - 1, 0), k + col_off))


def ffn_down(x, gu, conv_w, conv_b, w, *, seq, tm, tk):
    t, d = x.shape
    f = w.shape[0]
    assert t % tm == 0 and f % tk == 0 and seq % tm == 0
    nk = f // tk
    return pl.pallas_call(
        functools.partial(_ffn_down_body, seq_tiles=seq // tm),
        grid=(t // tm, nk),
        in_specs=[pl.BlockSpec((tm, d), lambda i, k: (i, 0)),
                  pl.BlockSpec((tm, tk), lambda i, k: (i, k)),
                  _halo_spec(tm, tk, 0),
                  pl.BlockSpec((tm, tk), lambda i, k: (i, k + nk)),
                  pl.BlockSpec((CONV_WIDTH, tk), lambda i, k: (0, k)),
                  pl.BlockSpec((1, tk), lambda i, k: (0, k)),
                  pl.BlockSpec((tk, d), lambda i, k: (k, 0))],
        out_specs=pl.BlockSpec((tm, d), lambda i, k: (i, 0)),
        out_shape=jax.ShapeDtypeStruct((t, d), F32),
        compiler_params=_params("parallel", "arbitrary"),
        name="ffn_down",
    )(x, gu, gu, gu, conv_w, conv_b.reshape(1, f), w)


def sconv_out(x, p, conv_w, w, *, seq, tm, tk):
    t, d = x.shape
    c = w.shape[0]
    assert t % tm == 0 and c % tk == 0 and seq % tm == 0
    nk = c // tk
    return pl.pallas_call(
        functools.partial(_sconv_out_body, seq_tiles=seq // tm),
        grid=(t // tm, nk),
        in_specs=[pl.BlockSpec((tm, d), lambda i, k: (i, 0)),
                  pl.BlockSpec((tm, tk), lambda i, k: (i, k)),
                  pl.BlockSpec((tm, tk), lambda i, k: (i, k + nk)),
                  _halo_spec(tm, tk, nk),
                  pl.BlockSpec((tm, tk), lambda i, k: (i, k + 2 * nk)),
                  _halo_spec(tm, tk, 2 * nk),
                  pl.BlockSpec((CONV_WIDTH, tk), lambda i, k: (0, k)),
                  pl.BlockSpec((tk, d), lambda i, k: (k, 0))],
        out_specs=pl.BlockSpec((tm, d), lambda i, k: (i, 0)),
        out_shape=jax.ShapeDtypeStruct((t, d), F32),
        compiler_params=_params("parallel", "arbitrary"),
        name="sconv_out",
    )(x, p, p, p, p, p, conv_w, w)


def _mix_out_body(x_ref, a_ref, b_ref, wa_ref, wb_ref, o_ref):
    o_ref[...] = (x_ref[...]
                  + jnp.dot(a_ref[...], wa_ref[...], preferred_element_type=F32)
                  + jnp.dot(b_ref[...], wb_ref[...], preferred_element_type=F32))


def mix_out(x, oa, ob, w, *, tm):
    t, d = x.shape
    ka, kb = oa.shape[1], ob.shape[1]
    assert ka == kb and w.shape[0] == ka + kb and t % tm == 0
    return pl.pallas_call(
        _mix_out_body,
        grid=(t // tm,),
        in_specs=[pl.BlockSpec((tm, d), lambda i: (i, 0)),
                  pl.BlockSpec((tm, ka), lambda i: (i, 0)),
                  pl.BlockSpec((tm, kb), lambda i: (i, 0)),
                  pl.BlockSpec((ka, d), lambda i: (0, 0)),
                  pl.BlockSpec((kb, d), lambda i: (1, 0))],
        out_specs=pl.BlockSpec((tm, d), lambda i: (i, 0)),
        out_shape=jax.ShapeDtypeStruct((t, d), F32),
        compiler_params=_params("parallel"),
        name="mix_out",
    )(x, oa, ob, w, w)


def _gmlp_body(u_ref, v_ref, gain_ref, ws_ref, bs_ref, o_ref, *, chunks):
    c = GMLP_CHUNK
    v = jax.nn.gelu(v_ref[...].astype(F32))
    vc = v - jnp.mean(v, axis=-1, keepdims=True)
    vn = vc * lax.rsqrt(jnp.mean(vc * vc, axis=-1, keepdims=True) + EPS) * gain_ref[...]
    vn = vn.astype(BF16)
    row = lax.broadcasted_iota(jnp.int32, (c, c), 0)
    col = lax.broadcasted_iota(jnp.int32, (c, c), 1)
    w = jnp.where(col <= row, ws_ref[...], 0.0).astype(BF16)
    bias = bs_ref[...]
    for ci in range(chunks):
        rows = slice(ci * c, (ci + 1) * c)
        mixed = jnp.dot(w, vn[rows, :], preferred_element_type=F32) + bias
        u = jax.nn.gelu(u_ref[rows, :].astype(F32))
        o_ref[rows, :] = (u * mixed).astype(o_ref.dtype)


def gmlp(p, gain, ws, bs, *, u_blk, v_blk, tt):
    t = p.shape[0]
    groups, c, _ = ws.shape
    gd = GMLP_GROUP_DIM
    assert t % tt == 0 and tt % c == 0
    return pl.pallas_call(
        functools.partial(_gmlp_body, chunks=tt // c),
        grid=(t // tt, groups),
        in_specs=[pl.BlockSpec((tt, gd), lambda i, g: (i, u_blk + g)),
                  pl.BlockSpec((tt, gd), lambda i, g: (i, v_blk + g)),
                  pl.BlockSpec((None, 1, gd), lambda i, g: (g, 0, 0)),
                  pl.BlockSpec((None, c, c), lambda i, g: (g, 0, 0)),
                  pl.BlockSpec((None, c, 1), lambda i, g: (g, 0, 0))],
        out_specs=pl.BlockSpec((tt, gd), lambda i, g: (i, g)),
        out_shape=jax.ShapeDtypeStruct((t, groups * gd), BF16),
        compiler_params=_params("parallel", "arbitrary"),
        name="gmlp",
    )(p, p, gain.reshape(groups, 1, gd), ws, bs.reshape(groups, c, 1))


def _compress_body(hk_ref, hv_ref, pe_ref, kw1_ref, kw2_ref, vw1_ref, vw2_ref, kg_ref, kc_ref, vc_ref):
    n = hk_ref.shape[0]

    def mlp(h_ref, w1_ref, w2_ref):
        h = h_ref[...].astype(F32)
        a = jnp.dot((h + pe_ref[0:1, :]).astype(BF16), w1_ref[0], preferred_element_type=F32)
        b = jnp.dot((h + pe_ref[1:2, :]).astype(BF16), w1_ref[1], preferred_element_type=F32)
        hid = jax.nn.gelu(a + pltpu.roll(b, n - 1, axis=0))
        return jnp.dot(hid.astype(BF16), w2_ref[...], preferred_element_type=F32)

    kc = mlp(hk_ref, kw1_ref, kw2_ref)
    ms = jnp.mean(kc * kc, axis=-1, keepdims=True)
    kc_ref[...] = (kc * lax.rsqrt(ms + EPS) * kg_ref[...]).astype(kc_ref.dtype)
    vc_ref[...] = mlp(hv_ref, vw1_ref, vw2_ref).astype(vc_ref.dtype)


def compress(hk, hv, pe2, kw1, kw2, vw1, vw2, kgain):
    b, g, n, hd = hk.shape
    d = HEAD_DIM
    half_spec = pl.BlockSpec((None, None, n, hd), lambda i, j: (i, j, 0, 0))
    out_spec = pl.BlockSpec((None, None, n, d), lambda i, j: (i, j, 0, 0))
    full = lambda shape: pl.BlockSpec(shape, lambda i, j: (0,) * len(shape))
    return pl.pallas_call(
        _compress_body,
        grid=(b, g),
        in_specs=[half_spec, half_spec, full((2, hd)), full((2, hd, d)), full((d, d)),
                  full((2, hd, d)), full((d, d)), full((1, d))],
        out_specs=[out_spec, out_spec],
        out_shape=[jax.ShapeDtypeStruct((b, g, n, d), BF16)] * 2,
        compiler_params=_params("parallel", "arbitrary"),
        name="nsa_compress",
    )(hk, hv, pe2, kw1, kw2, vw1, vw2, kgain)


def _dot_nt(a, b):
    return lax.dot_general(a, b, (((1,), (1,)), ((), ())), preferred_element_type=F32)


def _rms(x, gain):
    return x * lax.rsqrt(jnp.mean(x * x, axis=-1, keepdims=True) + EPS) * gain


def _nsa_body(q_ref, kc_ref, vc_ref, ks_ref, vs_ref, kw_ref, vw_ref, gl_ref, qg_ref, kg_ref, o_ref,
              ksn_ref, kwn_ref, m_ref, l_ref, acc_ref, *, tq, tk, n_slc, n_top):
    r_, d = NSA_GROUP, HEAD_DIM
    rows = r_ * tq
    nc = kc_ref.shape[0]
    qi = pl.program_id(2)
    q0 = qi * tq

    @pl.when(qi == 0)
    def _():
        ksn_ref[...] = _rms(ks_ref[...].astype(F32), kg_ref[1:2, :]).astype(BF16)
        kwn_ref[...] = _rms(kw_ref[...].astype(F32), kg_ref[2:3, :]).astype(BF16)

    qraw = q_ref[...].astype(F32)
    q = jnp.concatenate(
        [_rms(qraw[:, r * d:(r + 1) * d], qg_ref[...]) * (d ** -0.5) for r in range(r_)],
        axis=0).astype(BF16)

    def masked(s, mask, fill=NEG):
        return jnp.where(mask[None], s.reshape(r_, tq, s.shape[-1]), fill).reshape(s.shape)

    pos_c = q0 + lax.broadcasted_iota(jnp.int32, (tq, nc), 0)
    end_c = lax.broadcasted_iota(jnp.int32, (tq, nc), 1) * CMP_STRIDE + (CMP_BLOCK - 1)
    mask_c = end_c <= pos_c
    sc = masked(_dot_nt(q, kc_ref[...]), mask_c)
    mc = jnp.max(sc, axis=-1, keepdims=True)
    pc = masked(jnp.exp(sc - mc), mask_c, 0.0)
    pc = pc / jnp.maximum(jnp.sum(pc, axis=-1, keepdims=True), 1e-30)
    o_cmp = jnp.dot(pc.astype(BF16), vc_ref[...], preferred_element_type=F32)
    psum = pc[0:tq]
    for r in range(1, r_):
        psum = psum + pc[r * tq:(r + 1) * tq]

    j_o = lax.broadcasted_iota(jnp.int32, (n_slc, nc), 0) * SLC_BLOCK
    n_o = lax.broadcasted_iota(jnp.int32, (n_slc, nc), 1) * CMP_STRIDE
    ov_t = ((n_o < j_o + SLC_BLOCK) & (n_o + CMP_BLOCK > j_o)).astype(BF16)
    hi = psum.astype(BF16)
    r1 = psum - hi.astype(F32)
    lo = r1.astype(BF16)
    lo2 = (r1 - lo.astype(F32)).astype(BF16)
    imp_t = _dot_nt(ov_t, hi) + _dot_nt(ov_t, lo) + _dot_nt(ov_t, lo2)

    j_s = lax.broadcasted_iota(jnp.int32, (n_slc, tq), 0)
    pos_s = q0 + lax.broadcasted_iota(jnp.int32, (n_slc, tq), 1)
    dlt = pos_s // SLC_BLOCK - j_s
    forced = (j_s == 0) | ((dlt >= 0) & (dlt < N_LOCAL_SLC))
    val = jnp.where(j_s * SLC_BLOCK <= pos_s, imp_t, NEG)
    val = jnp.where(forced, FORCE_SCORE, val)
    cnt = jnp.zeros((n_slc, tq), jnp.int32)
    for i in range(n_slc):
        vi = val[i:i + 1, :]
        cnt = cnt + ((vi > val) | ((vi == val) & (j_s > i))).astype(jnp.int32)
    sel_t = (cnt < n_top).astype(F32)
    if n_slc < LANES:
        sel_t = jnp.concatenate([sel_t, jnp.zeros((LANES - n_slc, tq), F32)], axis=0)
    sel = sel_t.T.astype(BF16)
    n_sel = sel.shape[1]

    m_ref[...] = jnp.full((rows, 1), NEG, F32)
    l_ref[...] = jnp.zeros((rows, 1), F32)
    acc_ref[...] = jnp.zeros((rows, d), F32)
    qpos = q0 + lax.broadcasted_iota(jnp.int32, (tq, tk), 0)
    kloc = lax.broadcasted_iota(jnp.int32, (tq, tk), 1)
    j_e = lax.broadcasted_iota(jnp.int32, (n_sel, tk), 0)
    k_e = lax.broadcasted_iota(jnp.int32, (n_sel, tk), 1)

    def slc_tile(kt, carry):
        k0 = pl.multiple_of(kt * tk, tk)
        expand = (j_e == (k0 + k_e) // SLC_BLOCK).astype(BF16)
        picked = jnp.dot(sel, expand, preferred_element_type=F32)
        mask = (picked > 0.5) & (k0 + kloc <= qpos)
        s = masked(_dot_nt(q, ksn_ref[pl.ds(k0, tk), :]), mask)
        m_old = m_ref[...]
        m_new = jnp.maximum(m_old, jnp.max(s, axis=-1, keepdims=True))
        alpha = jnp.exp(m_old - m_new)
        p = jnp.exp(s - m_new)
        l_ref[...] = alpha * l_ref[...] + jnp.sum(p, axis=-1, keepdims=True)
        acc_ref[...] = alpha * acc_ref[...] + jnp.dot(
            p.astype(BF16), vs_ref[pl.ds(k0, tk), :], preferred_element_type=F32)
        m_ref[...] = m_new
        return carry

    lax.fori_loop(0, (q0 + tq + tk - 1) // tk, slc_tile, 0)
    o_slc = acc_ref[...] / l_ref[...]

    kw_len = WINDOW + tq
    w0 = pl.multiple_of(jnp.maximum(q0 - WINDOW, 0), tq)
    kpos_w = w0 + lax.broadcasted_iota(jnp.int32, (tq, kw_len), 1)
    qpos_w = q0 + lax.broadcasted_iota(jnp.int32, (tq, kw_len), 0)
    mask_w = (kpos_w <= qpos_w) & (qpos_w - kpos_w < WINDOW)
    sw = masked(_dot_nt(q, kwn_ref[pl.ds(w0, kw_len), :]), mask_w)
    pw = jnp.exp(sw - jnp.max(sw, axis=-1, keepdims=True))
    o_win = jnp.dot(pw.astype(BF16), vw_ref[pl.ds(w0, kw_len), :], preferred_element_type=F32)
    o_win = o_win / jnp.sum(pw, axis=-1, keepdims=True)

    gate = jax.nn.sigmoid(gl_ref[...].astype(F32))
    for r in range(r_):
        sl = slice(r * tq, (r + 1) * tq)
        c = N_BRANCH * r
        o = (gate[:, c:c + 1] * o_cmp[sl] + gate[:, c + 1:c + 2] * o_slc[sl]
             + gate[:, c + 2:c + 3] * o_win[sl])
        o_ref[:, r * d:(r + 1) * d] = o.astype(o_ref.dtype)


def nsa_attention(p, kc, vc, q_gain, k_gain, *, batch, seq, col, tq, tk):
    b, g, nc, d = kc.shape
    r_ = NSA_GROUP
    nq = seq // tq
    n_slc = seq // SLC_BLOCK
    assert seq % tq == 0 and seq % tk == 0 and tq % tk == 0 and WINDOW % tq == 0
    assert WINDOW + tq <= seq and n_slc <= LANES
    rows = r_ * tq
    kv_spec = lambda blk: pl.BlockSpec((seq, d), lambda bi, gi, qi: (bi, blk + gi))
    cmp_spec = pl.BlockSpec((None, None, nc, d), lambda bi, gi, qi: (bi, gi, 0, 0))
    return pl.pallas_call(
        functools.partial(_nsa_body, tq=tq, tk=tk, n_slc=n_slc, n_top=min(N_SLC, n_slc)),
        grid=(batch, g, nq),
        in_specs=[pl.BlockSpec((tq, r_ * d), lambda bi, gi, qi: (bi * nq + qi, col["q"] // r_ + gi)),
                  cmp_spec, cmp_spec,
                  kv_spec(col["ks"]), kv_spec(col["vs"]), kv_spec(col["kw"]), kv_spec(col["vw"]),
                  pl.BlockSpec((tq, LANES), lambda bi, gi, qi: (bi * nq + qi, col["gate"] + gi)),
                  pl.BlockSpec((1, d), lambda bi, gi, qi: (0, 0)),
                  pl.BlockSpec((N_BRANCH, d), lambda bi, gi, qi: (0, 0))],
        out_specs=pl.BlockSpec((tq, r_ * d), lambda bi, gi, qi: (bi * nq + qi, gi)),
        out_shape=jax.ShapeDtypeStruct((batch * seq, g * r_ * d), BF16),
        scratch_shapes=[pltpu.VMEM((seq, d), BF16), pltpu.VMEM((seq, d), BF16),
                        pltpu.VMEM((rows, 1), F32), pltpu.VMEM((rows, 1), F32),
                        pltpu.VMEM((rows, d), F32)],
        compiler_params=_params("parallel", "parallel", "arbitrary"),
        name="nsa_attention",
    )(p, kc, vc, p, p, p, p, p, q_gain.reshape(1, d), k_gain)


def _pad_cols(w, n):
    return jnp.pad(w, ((0, 0), (0, n - w.shape[1])))


def _even_layer(x, batch, seq, norm_g, w_in, q_gain, k_gain, cmp_pe, kw1, kw2, vw1, vw2,
                gmlp_norm, gmlp_ws, gmlp_b, w_out):
    d = HEAD_DIM
    dm = x.shape[1]
    gmlp_w = gmlp_ws.shape[0] * GMLP_GROUP_DIM
    qw = dm - gmlp_w
    g_kv = max(1, qw // d // NSA_GROUP)
    kv = g_kv * d
    n_kv = 2 * N_BRANCH
    n_gate = N_BRANCH * qw // d
    assert w_in.shape[1] == qw + n_kv * kv + n_gate + 2 * gmlp_w
    o_gl = qw + n_kv * kv
    o_u = o_gl + n_gate
    per_g = n_gate // g_kv
    gate_cols = [_pad_cols(w_in[:, o_gl + gi * per_g: o_gl + (gi + 1) * per_g], LANES) for gi in range(g_kv)]
    w_cat = jnp.concatenate([w_in[:, :o_gl], w_in[:, o_u:]] + gate_cols, axis=1)
    tn = 512
    n_pad = -(-w_cat.shape[1] // tn) * tn
    w_cat = _pad_cols(w_cat, n_pad).astype(BF16)
    col = {"q": 0, "kc": qw // d, "vc": (qw + kv) // d, "ks": (qw + 2 * kv) // d, "vs": (qw + 3 * kv) // d,
           "kw": (qw + 4 * kv) // d, "vw": (qw + 5 * kv) // d, "u": o_gl // d, "v": (o_gl + gmlp_w) // d,
           "gate": (o_gl + 2 * gmlp_w) // d}

    p = norm_matmul(x, norm_g, w_cat, tm=1024, tn=tn)

    nh = seq // CMP_STRIDE

    def halves(c0):
        h = p[:, c0 * d: c0 * d + kv].reshape(batch, seq, g_kv, d)
        return h.transpose(0, 2, 1, 3).reshape(batch, g_kv, nh, CMP_STRIDE * d)

    hd = CMP_STRIDE * d
    kc, vc = compress(halves(col["kc"]), halves(col["vc"]), cmp_pe.reshape(2, hd),
                      kw1.reshape(2, hd, d).astype(BF16), kw2.astype(BF16),
                      vw1.reshape(2, hd, d).astype(BF16), vw2.astype(BF16), k_gain[0:1])
    oa = nsa_attention(p, kc, vc, q_gain, k_gain, batch=batch, seq=seq, col=col, tq=256, tk=256)
    ob = gmlp(p, gmlp_norm, gmlp_ws, gmlp_b, u_blk=col["u"], v_blk=col["v"], tt=512)
    return mix_out(x, oa, ob, w_out.astype(BF16), tm=256)


def _ffn(x, seq, norm_g, w_in, conv_w, conv_b, w_down):
    gu = norm_matmul(x, norm_g, w_in.astype(BF16), tm=1024, tn=512)
    return ffn_down(x, gu, conv_w, conv_b, w_down.astype(BF16), seq=seq, tm=512, tk=512)


def kernel(x, ev_norm, ev_w_in, ev_q_gain, ev_k_gain, ev_cmp_pe, ev_cmp_k_w1, ev_cmp_k_w2, ev_cmp_v_w1, ev_cmp_v_w2, ev_gmlp_norm, ev_gmlp_ws, ev_gmlp_b, ev_w_out, od_norm, od_w_in, od_conv_w, od_w_out, ffn_norm, ffn_w_in, ffn_conv_w, ffn_conv_b, ffn_w_down):
    batch, seq, dm = x.shape
    depth = ffn_norm.shape[0]
    h = x.reshape(batch * seq, dm)
    for i in range(depth):
        j = i // 2
        if i % 2 == 0:
            h = _even_layer(h, batch, seq, ev_norm[j], ev_w_in[j], ev_q_gain[j], ev_k_gain[j], ev_cmp_pe[j],
                            ev_cmp_k_w1[j], ev_cmp_k_w2[j], ev_cmp_v_w1[j], ev_cmp_v_w2[j],
                            ev_gmlp_norm[j], ev_gmlp_ws[j], ev_gmlp_b[j], ev_w_out[j])
        else:
            p = norm_matmul(h, od_norm[j], od_w_in[j].astype(BF16), tm=1024, tn=512)
            h = sconv_out(h, p, od_conv_w[j], od_w_out[j].astype(BF16), seq=seq, tm=512, tk=512)
        h = _ffn(h, seq, ffn_norm[i], ffn_w_in[i], ffn_conv_w[i], ffn_conv_b[i], ffn_w_down[i])
    return h.reshape(batch, seq, dm)
```

```python
import functools

import jax
import jax.numpy as jnp
from jax import lax
from jax.experimental import pallas as pl
from jax.experimental.pallas import tpu as pltpu

F32 = jnp.float32
BF16 = jnp.bfloat16

HEAD_DIM = 128
NSA_GROUP = 4
N_BRANCH = 3
CMP_BLOCK = 32
CMP_STRIDE = 16
SLC_BLOCK = 64
N_SLC = 16
N_LOCAL_SLC = 2
WINDOW = 512
FORCE_SCORE = 1e9
GMLP_GROUP_DIM = 128
GMLP_CHUNK = 128
CONV_WIDTH = 3
EPS = 1e-6
NEG = -1e30

LANES = 128
F32_SUBLANES = 8
VMEM_LIMIT = 56 * 1024 * 1024


def _params(*sem):
    return pltpu.CompilerParams(dimension_semantics=sem, vmem_limit_bytes=VMEM_LIMIT)


def _dot(a, b):
    return jnp.dot(a, b, preferred_element_type=F32)


def _rms(x, gain):
    return x * lax.rsqrt(jnp.mean(x * x, axis=-1, keepdims=True) + EPS) * gain


def _iota(shape, axis):
    return lax.broadcasted_iota(jnp.int32, shape, axis)


def _resident(shape, index_map):
    return pl.BlockSpec(shape, index_map, pipeline_mode=pl.Buffered(1))


def _norm_matmul_body(x_ref, g_ref, w_ref, o_ref, xn_ref):
    @pl.when(pl.program_id(1) == 0)
    def _():
        xn_ref[...] = _rms(x_ref[...], g_ref[...]).astype(BF16)

    o_ref[...] = _dot(xn_ref[...], w_ref[...]).astype(o_ref.dtype)


def norm_matmul(x, g, w, *, tm, tn):
    t, d = x.shape
    n = w.shape[1]
    assert t % tm == 0 and n % tn == 0
    return pl.pallas_call(
        _norm_matmul_body,
        grid=(t // tm, n // tn),
        in_specs=[pl.BlockSpec((tm, d), lambda i, j: (i, 0)),
                  _resident((1, d), lambda i, j: (0, 0)),
                  pl.BlockSpec((d, tn), lambda i, j: (0, j))],
        out_specs=pl.BlockSpec((tm, tn), lambda i, j: (i, j)),
        out_shape=jax.ShapeDtypeStruct((t, n), BF16),
        scratch_shapes=[pltpu.VMEM((tm, d), BF16)],
        compiler_params=_params("parallel", "arbitrary"),
        name="norm_matmul",
    )(x, g.reshape(1, d), w)


def _shift_rows(cur, tail, k):
    out = pltpu.roll(cur, k, axis=0)
    row = _iota(cur.shape, 0)
    for r in range(k):
        src = F32_SUBLANES - k + r
        out = jnp.where(row == r, tail[src:src + 1, :], out)
    return out


def _conv3(cur, tail, cw):
    return (cw[0:1, :] * _shift_rows(cur, tail, 2) + cw[1:2, :] * _shift_rows(cur, tail, 1)
            + cw[2:3, :] * cur)


def _gated_in_body(x_ref, gn_ref, *rest, mode, rc, seq_tiles):
    n_parts = 2 if mode == "ffn" else 3
    w_refs = rest[:n_parts]
    cw_ref, cb_ref, h_ref, xn_ref, tail_ref = rest[n_parts:]
    i, j = pl.program_id(0), pl.program_id(1)
    tm = x_ref.shape[0]

    @pl.when(j == 0)
    def _():
        xn_ref[...] = _rms(x_ref[...], gn_ref[...]).astype(BF16)

    @pl.when(i % seq_tiles == 0)
    def _():
        tail_ref[j] = jnp.zeros(tail_ref.shape[1:], F32)

    cw = cw_ref[...]
    tail = tail_ref[j]
    for c in range(tm // rc):
        rows = slice(c * rc, (c + 1) * rc)
        xc = xn_ref[rows, :]
        parts = [_dot(xc, w_ref[...]) for w_ref in w_refs]
        if mode == "ffn":
            g, u = parts
            h = jax.nn.silu(_conv3(g, tail, cw) + cb_ref[...]) * u
            tail = g[rc - F32_SUBLANES:, :]
        else:
            a, cc, dd = parts
            m = cc * dd
            h = a * _conv3(m, tail, cw)
            tail = m[rc - F32_SUBLANES:, :]
        h_ref[rows, :] = h.astype(h_ref.dtype)
    tail_ref[j] = tail


def gated_in_proj(x, g, w, conv_w, conv_b, *, mode, seq, tm, tn, rc):
    t, d = x.shape
    n_parts = 2 if mode == "ffn" else 3
    f = w.shape[1] // n_parts
    assert t % tm == 0 and f % tn == 0 and seq % tm == 0 and tm % rc == 0
    nj = f // tn
    w_specs = [pl.BlockSpec((d, tn), functools.partial(lambda i, j, p: (0, p * nj + j), p=p))
               for p in range(n_parts)]
    return pl.pallas_call(
        functools.partial(_gated_in_body, mode=mode, rc=rc, seq_tiles=seq // tm),
        grid=(t // tm, nj),
        in_specs=[pl.BlockSpec((tm, d), lambda i, j: (i, 0)),
                  _resident((1, d), lambda i, j: (0, 0))] + w_specs + [
                  pl.BlockSpec((CONV_WIDTH, tn), lambda i, j: (0, j)),
                  pl.BlockSpec((1, tn), lambda i, j: (0, j))],
        out_specs=pl.BlockSpec((tm, tn), lambda i, j: (i, j)),
        out_shape=jax.ShapeDtypeStruct((t, f), BF16),
        scratch_shapes=[pltpu.VMEM((tm, d), BF16), pltpu.VMEM((nj, F32_SUBLANES, tn), F32)],
        compiler_params=_params("arbitrary", "arbitrary"),
        name=mode + "_in_proj",
    )(x, g.reshape(1, d), *([w] * n_parts), conv_w, conv_b.reshape(1, f))


def _matmul_residual_body(x_ref, *rest):
    n = (len(rest) - 1) // 2
    h_refs, w_refs, o_ref = rest[:n], rest[n:2 * n], rest[2 * n]
    acc = x_ref[...]
    for h_ref, w_ref in zip(h_refs, w_refs):
        acc = acc + _dot(h_ref[...], w_ref[...])
    o_ref[...] = acc


def matmul_residual(x, hs, w, *, tm):
    t, d = x.shape
    k = hs[0].shape[1]
    assert all(h.shape == (t, k) for h in hs) and w.shape == (k * len(hs), d) and t % tm == 0
    h_specs = [pl.BlockSpec((tm, k), lambda i: (i, 0)) for _ in hs]
    w_specs = [_resident((k, d), functools.partial(lambda i, p: (p, 0), p=p)) for p in range(len(hs))]
    return pl.pallas_call(
        _matmul_residual_body,
        grid=(t // tm,),
        in_specs=[pl.BlockSpec((tm, d), lambda i: (i, 0))] + h_specs + w_specs,
        out_specs=pl.BlockSpec((tm, d), lambda i: (i, 0)),
        out_shape=jax.ShapeDtypeStruct((t, d), F32),
        compiler_params=_params("parallel"),
        name="out_proj",
    )(x, *hs, *([w] * len(hs)))


def _gmlp_body(u_ref, v_ref, gain_ref, ws_ref, bs_ref, o_ref, *, chunks):
    c = GMLP_CHUNK
    v = jax.nn.gelu(v_ref[...].astype(F32))
    vc = v - jnp.mean(v, axis=-1, keepdims=True)
    vn = vc * lax.rsqrt(jnp.mean(vc * vc, axis=-1, keepdims=True) + EPS) * gain_ref[...]
    vn = vn.astype(BF16)
    w = jnp.where(_iota((c, c), 1) <= _iota((c, c), 0), ws_ref[...], 0.0).astype(BF16)
    bias = bs_ref[...]
    for ci in range(chunks):
        rows = slice(ci * c, (ci + 1) * c)
        mixed = _dot(w, vn[rows, :]) + bias
        u = jax.nn.gelu(u_ref[rows, :].astype(F32))
        o_ref[rows, :] = (u * mixed).astype(o_ref.dtype)


def gmlp(p, gain, ws, bs, *, u_blk, v_blk, tt):
    t = p.shape[0]
    groups, c, _ = ws.shape
    gd = GMLP_GROUP_DIM
    assert t % tt == 0 and tt % c == 0
    return pl.pallas_call(
        functools.partial(_gmlp_body, chunks=tt // c),
        grid=(t // tt, groups),
        in_specs=[pl.BlockSpec((tt, gd), lambda i, g: (i, u_blk + g)),
                  pl.BlockSpec((tt, gd), lambda i, g: (i, v_blk + g)),
                  pl.BlockSpec((None, 1, gd), lambda i, g: (g, 0, 0)),
                  pl.BlockSpec((None, c, c), lambda i, g: (g, 0, 0)),
                  pl.BlockSpec((None, c, 1), lambda i, g: (g, 0, 0))],
        out_specs=pl.BlockSpec((tt, gd), lambda i, g: (i, g)),
        out_shape=jax.ShapeDtypeStruct((t, groups * gd), BF16),
        compiler_params=_params("parallel", "arbitrary"),
        name="gmlp",
    )(p, p, gain.reshape(groups, 1, gd), ws, bs.reshape(groups, c, 1))


def _compress_body(hk_ref, hv_ref, pe_ref, kw1_ref, kw2_ref, vw1_ref, vw2_ref, kg_ref, kc_ref, vct_ref):
    n = hk_ref.shape[0]

    def mlp(h_ref, w1_ref, w2_ref):
        h = h_ref[...].astype(F32)
        a = _dot((h + pe_ref[0:1, :]).astype(BF16), w1_ref[0])
        b = _dot((h + pe_ref[1:2, :]).astype(BF16), w1_ref[1])
        hid = jax.nn.gelu(a + pltpu.roll(b, n - 1, axis=0))
        return _dot(hid.astype(BF16), w2_ref[...])

    kc_ref[...] = _rms(mlp(hk_ref, kw1_ref, kw2_ref), kg_ref[...]).astype(kc_ref.dtype)
    vct_ref[...] = mlp(hv_ref, vw1_ref, vw2_ref).T.astype(vct_ref.dtype)


def compress(hk, hv, pe2, kw1, kw2, vw1, vw2, kgain):
    b, g, n, hd = hk.shape
    d = HEAD_DIM
    half_spec = pl.BlockSpec((None, None, n, hd), lambda i, j: (i, j, 0, 0))
    full = lambda shape: pl.BlockSpec(shape, lambda i, j: (0,) * len(shape))
    return pl.pallas_call(
        _compress_body,
        grid=(b, g),
        in_specs=[half_spec, half_spec, full((2, hd)), full((2, hd, d)), full((d, d)),
                  full((2, hd, d)), full((d, d)), full((1, d))],
        out_specs=[pl.BlockSpec((None, None, n, d), lambda i, j: (i, j, 0, 0)),
                   pl.BlockSpec((None, None, d, n), lambda i, j: (i, j, 0, 0))],
        out_shape=[jax.ShapeDtypeStruct((b, g, n, d), BF16), jax.ShapeDtypeStruct((b, g, d, n), BF16)],
        compiler_params=_params("parallel", "arbitrary"),
        name="nsa_compress",
    )(hk, hv, pe2, kw1, kw2, vw1, vw2, kgain)


def _col_softmax(s, mask):
    s = jnp.where(mask, s, NEG)
    p = jnp.where(mask, jnp.exp(s - jnp.max(s, axis=0, keepdims=True)), 0.0)
    return p, jnp.sum(p, axis=0, keepdims=True)


def _nsa_body(q_ref, kc_ref, vct_ref, ks_ref, vs_ref, kw_ref, vw_ref, gl_ref, qg_ref, kg_ref, o_ref,
              ksn_ref, kwn_ref, vst_ref, vwt_ref, m_ref, l_ref, acc_ref, *, tq, tk, n_slc, n_top):
    r_, d = NSA_GROUP, HEAD_DIM
    cols = r_ * tq
    nc = kc_ref.shape[0]
    seq = ks_ref.shape[0]
    qi = pl.program_id(2)
    q0 = qi * tq
    head = lambda r: slice(r * tq, (r + 1) * tq)

    @pl.when(qi == 0)
    def _():
        ksn_ref[...] = _rms(ks_ref[...].astype(F32), kg_ref[1:2, :]).astype(BF16)
        kwn_ref[...] = _rms(kw_ref[...].astype(F32), kg_ref[2:3, :]).astype(BF16)
        for c in range(seq // tk):
            vst_ref[c] = vs_ref[c * tk:(c + 1) * tk, :].astype(F32).T.astype(BF16)
        for c in range(seq // tq):
            vwt_ref[c] = vw_ref[c * tq:(c + 1) * tq, :].astype(F32).T.astype(BF16)

    qraw = q_ref[...].astype(F32)
    qt = jnp.concatenate(
        [(_rms(qraw[:, r * d:(r + 1) * d], qg_ref[...]) * (d ** -0.5)).T for r in range(r_)],
        axis=1).astype(BF16)

    sc = _dot(kc_ref[...], qt)
    mask_c = _iota((nc, tq), 0) * CMP_STRIDE + (CMP_BLOCK - 1) <= q0 + _iota((nc, tq), 1)
    o_cmp = []
    psum = None
    for r in range(r_):
        p, den = _col_softmax(sc[:, head(r)], mask_c)
        p = p / jnp.maximum(den, 1e-30)
        psum = p if psum is None else psum + p
        o_cmp.append(_dot(vct_ref[...], p.astype(BF16)))

    cmp_start = _iota((n_slc, nc), 1) * CMP_STRIDE
    slc_start = _iota((n_slc, nc), 0) * SLC_BLOCK
    overlap = ((cmp_start < slc_start + SLC_BLOCK) & (cmp_start + CMP_BLOCK > slc_start)).astype(BF16)
    hi = psum.astype(BF16)
    r1 = psum - hi.astype(F32)
    lo = r1.astype(BF16)
    lo2 = (r1 - lo.astype(F32)).astype(BF16)
    imp = _dot(overlap, hi) + _dot(overlap, lo) + _dot(overlap, lo2)

    j_s = _iota((n_slc, tq), 0)
    pos_s = q0 + _iota((n_slc, tq), 1)
    dlt = pos_s // SLC_BLOCK - j_s
    forced = (j_s == 0) | ((dlt >= 0) & (dlt < N_LOCAL_SLC))
    val = jnp.where(j_s * SLC_BLOCK <= pos_s, imp, NEG)
    val = jnp.where(forced, FORCE_SCORE, val)
    cnt = jnp.zeros((n_slc, tq), jnp.int32)
    for i in range(n_slc):
        vi = val[i:i + 1, :]
        cnt = cnt + ((vi > val) | ((vi == val) & (j_s > i))).astype(jnp.int32)
    sel = (cnt < n_top).astype(BF16)

    m_ref[...] = jnp.full((1, cols), NEG, F32)
    l_ref[...] = jnp.zeros((1, cols), F32)
    acc_ref[...] = jnp.zeros((d, cols), F32)
    causal_gap = q0 + _iota((tk, tq), 1) - _iota((tk, tq), 0)
    blk_gap = _iota((tk, n_slc), 1) * SLC_BLOCK - _iota((tk, n_slc), 0)

    def slc_tile(kt, carry):
        k0 = pl.multiple_of(kt * tk, tk)
        off = k0 - blk_gap
        expand = ((off >= 0) & (off < SLC_BLOCK)).astype(BF16)
        mask = (_dot(expand, sel) > 0.5) & (causal_gap >= k0)
        s_all = _dot(ksn_ref[pl.ds(k0, tk), :], qt)
        vt = vst_ref[kt]
        for r in range(r_):
            s = jnp.where(mask, s_all[:, head(r)], NEG)
            m_old = m_ref[:, head(r)]
            m_new = jnp.maximum(m_old, jnp.max(s, axis=0, keepdims=True))
            alpha = jnp.exp(m_old - m_new)
            p = jnp.exp(s - m_new)
            l_ref[:, head(r)] = alpha * l_ref[:, head(r)] + jnp.sum(p, axis=0, keepdims=True)
            acc_ref[:, head(r)] = alpha * acc_ref[:, head(r)] + _dot(vt, p.astype(BF16))
            m_ref[:, head(r)] = m_new
        return carry

    lax.fori_loop(0, (q0 + tq + tk - 1) // tk, slc_tile, 0)

    nw = WINDOW // tq + 1
    wc = jnp.maximum(qi - WINDOW // tq, 0)
    w0 = pl.multiple_of(wc * tq, tq)
    sw = _dot(kwn_ref[pl.ds(w0, nw * tq), :], qt)
    gap_w = q0 + _iota((nw * tq, tq), 1) - (w0 + _iota((nw * tq, tq), 0))
    mask_w = (gap_w >= 0) & (gap_w < WINDOW)
    o_win = []
    for r in range(r_):
        p, den = _col_softmax(sw[:, head(r)], mask_w)
        p = p.astype(BF16)
        o = _dot(vwt_ref[wc], p[0:tq, :])
        for c in range(1, nw):
            o = o + _dot(vwt_ref[wc + c], p[c * tq:(c + 1) * tq, :])
        o_win.append(o / den)

    gate = jax.nn.sigmoid(gl_ref[...].astype(F32)).T
    for r in range(r_):
        c = N_BRANCH * r
        o_slc = acc_ref[:, head(r)] / l_ref[:, head(r)]
        o = (gate[c:c + 1, :] * o_cmp[r] + gate[c + 1:c + 2, :] * o_slc
             + gate[c + 2:c + 3, :] * o_win[r])
        o_ref[:, r * d:(r + 1) * d] = o.T.astype(o_ref.dtype)


def nsa_attention(p, kc, vct, q_gain, k_gain, *, batch, seq, col, tq, tk):
    b, g, nc, d = kc.shape
    r_ = NSA_GROUP
    nq = seq // tq
    n_slc = seq // SLC_BLOCK
    assert seq % tq == 0 and seq % tk == 0 and WINDOW % tq == 0 and WINDOW + tq <= seq
    cols = r_ * tq
    kv_spec = lambda blk: pl.BlockSpec((seq, d), lambda bi, gi, qi: (bi, blk + gi))
    return pl.pallas_call(
        functools.partial(_nsa_body, tq=tq, tk=tk, n_slc=n_slc, n_top=min(N_SLC, n_slc)),
        grid=(batch, g, nq),
        in_specs=[pl.BlockSpec((tq, r_ * d), lambda bi, gi, qi: (bi * nq + qi, col["q"] // r_ + gi)),
                  pl.BlockSpec((None, None, nc, d), lambda bi, gi, qi: (bi, gi, 0, 0)),
                  pl.BlockSpec((None, None, d, nc), lambda bi, gi, qi: (bi, gi, 0, 0)),
                  kv_spec(col["ks"]), kv_spec(col["vs"]), kv_spec(col["kw"]), kv_spec(col["vw"]),
                  pl.BlockSpec((tq, LANES), lambda bi, gi, qi: (bi * nq + qi, col["gate"] + gi)),
                  pl.BlockSpec((1, d), lambda bi, gi, qi: (0, 0)),
                  pl.BlockSpec((N_BRANCH, d), lambda bi, gi, qi: (0, 0))],
        out_specs=pl.BlockSpec((tq, r_ * d), lambda bi, gi, qi: (bi * nq + qi, gi)),
        out_shape=jax.ShapeDtypeStruct((batch * seq, g * r_ * d), BF16),
        scratch_shapes=[pltpu.VMEM((seq, d), BF16), pltpu.VMEM((seq, d), BF16),
                        pltpu.VMEM((seq // tk, d, tk), BF16), pltpu.VMEM((seq // tq, d, tq), BF16),
                        pltpu.VMEM((1, cols), F32), pltpu.VMEM((1, cols), F32),
                        pltpu.VMEM((d, cols), F32)],
        compiler_params=_params("parallel", "parallel", "arbitrary"),
        name="nsa_attention",
    )(p, kc, vct, p, p, p, p, p, q_gain.reshape(1, d), k_gain)


def _pad_cols(w, n):
    return jnp.pad(w, ((0, 0), (0, n - w.shape[1])))


def _even_mixer(x, batch, seq, norm_g, w_in, q_gain, k_gain, cmp_pe, kw1, kw2, vw1, vw2,
                gmlp_norm, gmlp_ws, gmlp_b, w_out):
    d = HEAD_DIM
    dm = x.shape[1]
    gmlp_w = gmlp_ws.shape[0] * GMLP_GROUP_DIM
    qw = dm - gmlp_w
    g_kv = max(1, qw // d // NSA_GROUP)
    kv = g_kv * d
    n_kv = 2 * N_BRANCH
    n_gate = N_BRANCH * qw // d
    assert w_in.shape[1] == qw + n_kv * kv + n_gate + 2 * gmlp_w
    o_gl = qw + n_kv * kv
    o_u = o_gl + n_gate
    per_g = n_gate // g_kv
    gate_cols = [_pad_cols(w_in[:, o_gl + gi * per_g: o_gl + (gi + 1) * per_g], LANES) for gi in range(g_kv)]
    w_cat = jnp.concatenate([w_in[:, :o_gl], w_in[:, o_u:]] + gate_cols, axis=1)
    tn = 512
    w_cat = _pad_cols(w_cat, -(-w_cat.shape[1] // tn) * tn).astype(BF16)
    col = {"q": 0, "kc": qw // d, "vc": (qw + kv) // d, "ks": (qw + 2 * kv) // d, "vs": (qw + 3 * kv) // d,
           "kw": (qw + 4 * kv) // d, "vw": (qw + 5 * kv) // d, "u": o_gl // d, "v": (o_gl + gmlp_w) // d,
           "gate": (o_gl + 2 * gmlp_w) // d}

    p = norm_matmul(x, norm_g, w_cat, tm=1024, tn=tn)

    hd = CMP_STRIDE * d

    def halves(c0):
        h = p[:, c0 * d: c0 * d + kv].reshape(batch, seq, g_kv, d)
        return h.transpose(0, 2, 1, 3).reshape(batch, g_kv, seq // CMP_STRIDE, hd)

    kc, vct = compress(halves(col["kc"]), halves(col["vc"]), cmp_pe.reshape(2, hd),
                       kw1.reshape(2, hd, d).astype(BF16), kw2.astype(BF16),
                       vw1.reshape(2, hd, d).astype(BF16), vw2.astype(BF16), k_gain[0:1])
    oa = nsa_attention(p, kc, vct, q_gain, k_gain, batch=batch, seq=seq, col=col, tq=256, tk=256)
    ob = gmlp(p, gmlp_norm, gmlp_ws, gmlp_b, u_blk=col["u"], v_blk=col["v"], tt=512)
    return matmul_residual(x, [oa, ob], w_out.astype(BF16), tm=256)


def _conv_ffn(x, seq, norm_g, w_in, conv_w, conv_b, w_down):
    h = gated_in_proj(x, norm_g, w_in.astype(BF16), conv_w, conv_b, mode="ffn", seq=seq,
                      tm=1024, tn=512, rc=256)
    return matmul_residual(x, [h], w_down.astype(BF16), tm=256)


def _odd_mixer(x, seq, norm_g, w_in, conv_w, w_out):
    no_bias = jnp.zeros((conv_w.shape[1],), F32)
    h = gated_in_proj(x, norm_g, w_in.astype(BF16), conv_w, no_bias, mode="sconv", seq=seq,
                      tm=1024, tn=512, rc=256)
    return matmul_residual(x, [h], w_out.astype(BF16), tm=256)


def kernel(x, ev_norm, ev_w_in, ev_q_gain, ev_k_gain, ev_cmp_pe, ev_cmp_k_w1, ev_cmp_k_w2, ev_cmp_v_w1, ev_cmp_v_w2, ev_gmlp_norm, ev_gmlp_ws, ev_gmlp_b, ev_w_out, od_norm, od_w_in, od_conv_w, od_w_out, ffn_norm, ffn_w_in, ffn_conv_w, ffn_conv_b, ffn_w_down):
    batch, seq, dm = x.shape
    depth = ffn_norm.shape[0]
    h = x.reshape(batch * seq, dm)
    for i in range(depth):
        j = i // 2
        if i % 2 == 0:
            h = _even_mixer(h, batch, seq, ev_norm[j], ev_w_in[j], ev_q_gain[j], ev_k_gain[j], ev_cmp_pe[j],
                            ev_cmp_k_w1[j], ev_cmp_k_w2[j], ev_cmp_v_w1[j], ev_cmp_v_w2[j],
                            ev_gmlp_norm[j], ev_gmlp_ws[j], ev_gmlp_b[j], ev_w_out[j])
        else:
            h = _odd_mixer(h, seq, od_norm[j], od_w_in[j], od_conv_w[j], od_w_out[j])
        h = _conv_ffn(h, seq, ffn_norm[i], ffn_w_in[i], ffn_conv_w[i], ffn_conv_b[i], ffn_w_down[i])
    return h.reshape(batch, seq, dm)
```

```python
import functools

import jax
import jax.numpy as jnp
from jax import lax
from jax.experimental import pallas as pl
from jax.experimental.pallas import tpu as pltpu

F32 = jnp.float32
BF16 = jnp.bfloat16

HEAD_DIM = 128
NSA_GROUP = 4
N_BRANCH = 3
CMP_BLOCK = 32
CMP_STRIDE = 16
SLC_BLOCK = 64
SLC_SHIFT = SLC_BLOCK.bit_length() - 1
LOG2_E = 1.4426950408889634
QK_AHEAD = 2
N_SLC = 16
N_LOCAL_SLC = 2
WINDOW = 512
FORCE_SCORE = 1e9
GMLP_GROUP_DIM = 128
GMLP_CHUNK = 128
CONV_WIDTH = 3
EPS = 1e-6
NEG = -1e30

LANES = 128
F32_SUBLANES = 8
VMEM_LIMIT = 56 * 1024 * 1024


def _params(*sem):
    return pltpu.CompilerParams(dimension_semantics=sem, vmem_limit_bytes=VMEM_LIMIT)


def _dot(a, b):
    return jnp.dot(a, b, preferred_element_type=F32)


def _rms(x, gain):
    return x * lax.rsqrt(jnp.mean(x * x, axis=-1, keepdims=True) + EPS) * gain


def _iota(shape, axis):
    return lax.broadcasted_iota(jnp.int32, shape, axis)


def _resident(shape, index_map):
    return pl.BlockSpec(shape, index_map, pipeline_mode=pl.Buffered(1))


def _norm_matmul_body(x_ref, g_ref, w_ref, o_ref, xn_ref):
    @pl.when(pl.program_id(1) == 0)
    def _():
        xn_ref[...] = _rms(x_ref[...], g_ref[...]).astype(BF16)

    o_ref[...] = _dot(xn_ref[...], w_ref[...]).astype(o_ref.dtype)


def norm_matmul(x, g, w, *, tm, tn):
    t, d = x.shape
    n = w.shape[1]
    assert t % tm == 0 and n % tn == 0
    return pl.pallas_call(
        _norm_matmul_body,
        grid=(t // tm, n // tn),
        in_specs=[pl.BlockSpec((tm, d), lambda i, j: (i, 0)),
                  _resident((1, d), lambda i, j: (0, 0)),
                  pl.BlockSpec((d, tn), lambda i, j: (0, j))],
        out_specs=pl.BlockSpec((tm, tn), lambda i, j: (i, j)),
        out_shape=jax.ShapeDtypeStruct((t, n), BF16),
        scratch_shapes=[pltpu.VMEM((tm, d), BF16)],
        compiler_params=_params("parallel", "arbitrary"),
        name="norm_matmul",
    )(x, g.reshape(1, d), w)


def _shift_rows(cur, tail, k):
    out = pltpu.roll(cur, k, axis=0)
    row = _iota(cur.shape, 0)
    for r in range(k):
        src = F32_SUBLANES - k + r
        out = jnp.where(row == r, tail[src:src + 1, :], out)
    return out


def _conv3(cur, tail, cw):
    return (cw[0:1, :] * _shift_rows(cur, tail, 2) + cw[1:2, :] * _shift_rows(cur, tail, 1)
            + cw[2:3, :] * cur)


def _gated_in_body(x_ref, gn_ref, *rest, mode, rc, seq_tiles):
    n_parts = 2 if mode == "ffn" else 3
    w_refs = rest[:n_parts]
    cw_ref, cb_ref, h_ref, xn_ref, tail_ref = rest[n_parts:]
    i, j = pl.program_id(0), pl.program_id(1)
    tm = x_ref.shape[0]

    @pl.when(j == 0)
    def _():
        xn_ref[...] = _rms(x_ref[...], gn_ref[...]).astype(BF16)

    @pl.when(i % seq_tiles == 0)
    def _():
        tail_ref[j] = jnp.zeros(tail_ref.shape[1:], F32)

    cw = cw_ref[...]
    tail = tail_ref[j]
    for c in range(tm // rc):
        rows = slice(c * rc, (c + 1) * rc)
        xc = xn_ref[rows, :]
        parts = [_dot(xc, w_ref[...]) for w_ref in w_refs]
        if mode == "ffn":
            g, u = parts
            h = jax.nn.silu(_conv3(g, tail, cw) + cb_ref[...]) * u
            tail = g[rc - F32_SUBLANES:, :]
        else:
            a, cc, dd = parts
            m = cc * dd
            h = a * _conv3(m, tail, cw)
            tail = m[rc - F32_SUBLANES:, :]
        h_ref[rows, :] = h.astype(h_ref.dtype)
    tail_ref[j] = tail


def gated_in_proj(x, g, w, conv_w, conv_b, *, layer, mode, seq, tm, tn, rc):
    t, d = x.shape
    n_parts = 2 if mode == "ffn" else 3
    f = w.shape[2] // n_parts
    assert t % tm == 0 and f % tn == 0 and seq % tm == 0 and tm % rc == 0
    nj = f // tn
    w_specs = [pl.BlockSpec((None, d, tn), functools.partial(lambda i, j, p: (layer, 0, p * nj + j), p=p))
               for p in range(n_parts)]
    return pl.pallas_call(
        functools.partial(_gated_in_body, mode=mode, rc=rc, seq_tiles=seq // tm),
        grid=(t // tm, nj),
        in_specs=[pl.BlockSpec((tm, d), lambda i, j: (i, 0)),
                  _resident((None, 1, d), lambda i, j: (layer, 0, 0))] + w_specs + [
                  pl.BlockSpec((None, CONV_WIDTH, tn), lambda i, j: (layer, 0, j)),
                  pl.BlockSpec((None, 1, tn), lambda i, j: (layer, 0, j))],
        out_specs=pl.BlockSpec((tm, tn), lambda i, j: (i, j)),
        out_shape=jax.ShapeDtypeStruct((t, f), BF16),
        scratch_shapes=[pltpu.VMEM((tm, d), BF16), pltpu.VMEM((nj, F32_SUBLANES, tn), F32)],
        compiler_params=_params("arbitrary", "arbitrary"),
        name=mode + "_in_proj",
    )(x, g[:, None, :], *([w] * n_parts), conv_w, conv_b[:, None, :])


def _matmul_residual_body(x_ref, *rest):
    n = (len(rest) - 1) // 2
    h_refs, w_refs, o_ref = rest[:n], rest[n:2 * n], rest[2 * n]
    acc = x_ref[...]
    for h_ref, w_ref in zip(h_refs, w_refs):
        acc = acc + _dot(h_ref[...], w_ref[...])
    o_ref[...] = acc


def matmul_residual(x, hs, w, *, layer, tm):
    t, d = x.shape
    k = hs[0].shape[1]
    assert all(h.shape == (t, k) for h in hs) and w.shape[1:] == (k * len(hs), d) and t % tm == 0
    h_specs = [pl.BlockSpec((tm, k), lambda i: (i, 0)) for _ in hs]
    w_specs = [_resident((None, k, d), functools.partial(lambda i, p: (layer, p, 0), p=p))
               for p in range(len(hs))]
    return pl.pallas_call(
        _matmul_residual_body,
        grid=(t // tm,),
        in_specs=[pl.BlockSpec((tm, d), lambda i: (i, 0))] + h_specs + w_specs,
        out_specs=pl.BlockSpec((tm, d), lambda i: (i, 0)),
        out_shape=jax.ShapeDtypeStruct((t, d), F32),
        compiler_params=_params("parallel"),
        name="out_proj",
    )(x, *hs, *([w] * len(hs)))


def _gmlp_body(u_ref, v_ref, gain_ref, ws_ref, bs_ref, o_ref, *, chunks):
    c, gd = GMLP_CHUNK, GMLP_GROUP_DIM
    causal = _iota((c, c), 1) <= _iota((c, c), 0)
    for g in range(ws_ref.shape[0]):
        cols = slice(g * gd, (g + 1) * gd)
        v = jax.nn.gelu(v_ref[:, cols].astype(F32))
        vc = v - jnp.mean(v, axis=-1, keepdims=True)
        vn = vc * lax.rsqrt(jnp.mean(vc * vc, axis=-1, keepdims=True) + EPS) * gain_ref[g]
        vn = vn.astype(BF16)
        w = jnp.where(causal, ws_ref[g], 0.0).astype(BF16)
        bias = bs_ref[g]
        for ci in range(chunks):
            rows = slice(ci * c, (ci + 1) * c)
            mixed = _dot(w, vn[rows, :]) + bias
            u = jax.nn.gelu(u_ref[rows, cols].astype(F32))
            o_ref[rows, cols] = (u * mixed).astype(o_ref.dtype)


def gmlp(p, gain, ws, bs, *, u_blk, v_blk, tt):
    t = p.shape[0]
    groups, c, _ = ws.shape
    gd = GMLP_GROUP_DIM
    width = groups * gd
    assert t % tt == 0 and tt % c == 0
    full = lambda shape: _resident(shape, lambda i: (0,) * len(shape))
    return pl.pallas_call(
        functools.partial(_gmlp_body, chunks=tt // c),
        grid=(t // tt,),
        in_specs=[pl.BlockSpec((tt, width), lambda i: (i, u_blk)),
                  pl.BlockSpec((tt, width), lambda i: (i, v_blk)),
                  full((groups, 1, gd)), full((groups, c, c)), full((groups, c, 1))],
        out_specs=pl.BlockSpec((tt, width), lambda i: (i, 0)),
        out_shape=jax.ShapeDtypeStruct((t, width), BF16),
        compiler_params=_params("parallel"),
        name="gmlp",
    )(p, p, gain.reshape(groups, 1, gd), ws, bs.reshape(groups, c, 1))


def _compress_body(k_ref, v_ref, pe_ref, kw1_ref, kw2_ref, vw1_ref, vw2_ref, kg_ref, kc_ref, vct_ref, x_ref):
    st = CMP_STRIDE
    n = x_ref.shape[0] // st

    def mlp(src_ref, w1_ref, w2_ref):
        x_ref[...] = src_ref[...].astype(F32)

        def half(part):
            return jnp.concatenate(
                [(x_ref[pl.ds(l, n, stride=st), :] + pe_ref[part * st + l:part * st + l + 1, :]).astype(BF16)
                 for l in range(st)], axis=1)

        a = _dot(half(0), w1_ref[0])
        b = _dot(half(1), w1_ref[1])
        hid = jax.nn.gelu(a + pltpu.roll(b, n - 1, axis=0))
        return _dot(hid.astype(BF16), w2_ref[...])

    kc_ref[...] = _rms(mlp(k_ref, kw1_ref, kw2_ref), kg_ref[...]).astype(kc_ref.dtype)
    vct_ref[...] = mlp(v_ref, vw1_ref, vw2_ref).T.astype(vct_ref.dtype)


def compress(p, pe, kw1, kw2, vw1, vw2, kgain, *, batch, seq, g_kv, k_blk, v_blk):
    d = HEAD_DIM
    st = CMP_STRIDE
    n = seq // st
    hd = st * d
    assert CMP_BLOCK == 2 * st and pe.shape == (CMP_BLOCK, d) and kw1.shape == (2, hd, d)
    full = lambda shape: pl.BlockSpec(shape, lambda i, j: (0,) * len(shape))
    return pl.pallas_call(
        _compress_body,
        grid=(batch, g_kv),
        in_specs=[pl.BlockSpec((seq, d), lambda i, j: (i, k_blk + j)),
                  pl.BlockSpec((seq, d), lambda i, j: (i, v_blk + j)),
                  full((CMP_BLOCK, d)), full((2, hd, d)), full((d, d)),
                  full((2, hd, d)), full((d, d)), full((1, d))],
        out_specs=[pl.BlockSpec((None, None, n, d), lambda i, j: (i, j, 0, 0)),
                   pl.BlockSpec((None, None, d, n), lambda i, j: (i, j, 0, 0))],
        out_shape=[jax.ShapeDtypeStruct((batch, g_kv, n, d), BF16),
                   jax.ShapeDtypeStruct((batch, g_kv, d, n), BF16)],
        scratch_shapes=[pltpu.VMEM((seq, d), F32)],
        compiler_params=_params("parallel", "arbitrary"),
        name="nsa_compress",
    )(p, p, pe, kw1, kw2, vw1, vw2, kgain)


def _col_softmax(s, mask, *, may_be_empty):
    s = jnp.where(mask, s, NEG)
    p = jnp.exp2(s - jnp.max(s, axis=0, keepdims=True))
    if may_be_empty:
        p = jnp.where(mask, p, 0.0)
    return p, jnp.sum(p, axis=0, keepdims=True)


def _nsa_body(q_ref, kc_ref, vct_ref, ov_ref, ks_ref, vs_ref, kw_ref, vw_ref, gl_ref, qg_ref, kg_ref, o_ref,
              ksx_ref, kwn_ref, vst_ref, vwt_ref, m_ref, l_ref, acc_ref, *, tq, tk, n_slc, n_top):
    r_, d = NSA_GROUP, HEAD_DIM
    cols = r_ * tq
    nc = kc_ref.shape[0]
    seq = ks_ref.shape[0]
    qi = pl.program_id(2)
    q0 = qi * tq
    head = lambda r: slice(r * tq, (r + 1) * tq)

    @pl.when(qi == 0)
    def _():
        ksx_ref[:, 0:d] = _rms(ks_ref[...].astype(F32), kg_ref[1:2, :]).astype(BF16)
        ksx_ref[:, d:2 * d] = (_iota((seq, d), 0) >> SLC_SHIFT == _iota((seq, d), 1)).astype(BF16)
        kwn_ref[...] = _rms(kw_ref[...].astype(F32), kg_ref[2:3, :]).astype(BF16)
        for c in range(seq // tk):
            vst_ref[c] = vs_ref[c * tk:(c + 1) * tk, :].astype(F32).T.astype(BF16)
        for c in range(seq // tq):
            vwt_ref[c] = vw_ref[c * tq:(c + 1) * tq, :].astype(F32).T.astype(BF16)

    qraw = q_ref[...].astype(F32)
    qt = jnp.concatenate(
        [(_rms(qraw[:, r * d:(r + 1) * d], qg_ref[...]) * (d ** -0.5 * LOG2_E)).T for r in range(r_)],
        axis=1).astype(BF16)

    sc = _dot(kc_ref[...], qt)
    mask_c = _iota((nc, tq), 0) * CMP_STRIDE + (CMP_BLOCK - 1) <= q0 + _iota((nc, tq), 1)
    o_cmp = []
    psum = None
    for r in range(r_):
        p, den = _col_softmax(sc[:, head(r)], mask_c, may_be_empty=True)
        p = p / jnp.maximum(den, 1e-30)
        psum = p if psum is None else psum + p
        o_cmp.append(_dot(vct_ref[...], p.astype(BF16)))

    overlap = ov_ref[...]
    hi = psum.astype(BF16)
    r1 = psum - hi.astype(F32)
    lo = r1.astype(BF16)
    lo2 = (r1 - lo.astype(F32)).astype(BF16)
    imp = _dot(overlap, hi) + _dot(overlap, lo) + _dot(overlap, lo2)

    j_s = _iota((n_slc, tq), 0)
    pos_s = q0 + _iota((n_slc, tq), 1)
    dlt = (pos_s >> SLC_SHIFT) - j_s
    forced = (j_s == 0) | ((dlt >= 0) & (dlt < N_LOCAL_SLC))
    val = jnp.where(dlt >= 0, imp, NEG)
    val = jnp.where(forced, FORCE_SCORE, val)
    cnt = jnp.zeros((n_slc, tq), jnp.int32)
    for i in range(n_slc):
        vi = val[i:i + 1, :]
        cnt = cnt + ((vi > val) | ((vi == val) & (j_s > i))).astype(jnp.int32)
    bias = jnp.where(cnt < n_top, 0.0, NEG)
    if n_slc < d:
        bias = jnp.concatenate([bias, jnp.zeros((d - n_slc, tq), F32)], axis=0)
    qx = jnp.concatenate([qt, jnp.concatenate([bias.astype(BF16)] * r_, axis=1)], axis=0)

    m_ref[...] = jnp.full((1, cols), NEG, F32)
    l_ref[...] = jnp.zeros((1, cols), F32)
    acc_ref[...] = jnp.zeros((d, cols), F32)

    def scores(kt):
        k0 = pl.multiple_of(kt * tk, tk)
        return _dot(ksx_ref[pl.ds(k0, tk), :], qx)

    def fold(kt, r, s, causal):
        if causal is not None:
            s = jnp.where(causal >= kt * tk, s, NEG)
        m_old = m_ref[:, head(r)]
        m_new = jnp.maximum(m_old, jnp.max(s, axis=0, keepdims=True))
        alpha = jnp.exp2(m_old - m_new)
        p = jnp.exp2(s - m_new)
        l_ref[:, head(r)] = alpha * l_ref[:, head(r)] + jnp.sum(p, axis=0, keepdims=True)
        acc_ref[:, head(r)] = alpha * acc_ref[:, head(r)] + _dot(vst_ref[kt], p.astype(BF16))
        m_ref[:, head(r)] = m_new

    def slc_tiles(kts, causal):
        s_all = [scores(kt) for kt in kts]
        for kt, s in zip(kts, s_all):
            for r in range(r_):
                fold(kt, r, s[:, head(r)], causal)

    n_quads = q0 // (4 * tk)

    def quad(i, carry):
        slc_tiles([4 * i + dd for dd in range(4)], None)
        return carry

    lax.fori_loop(0, n_quads, quad, 0)
    causal_gap = q0 + _iota((tk, tq), 1) - _iota((tk, tq), 0)
    n_tiles = (q0 + tq + tk - 1) // tk

    def causal_pair(i, carry):
        slc_tiles([4 * n_quads + 2 * i, 4 * n_quads + 2 * i + 1], causal_gap)
        return carry

    lax.fori_loop(0, (n_tiles - 4 * n_quads + 1) // 2, causal_pair, 0)

    nw = WINDOW // tq + 1
    wc = jnp.maximum(qi - WINDOW // tq, 0)
    w0 = pl.multiple_of(wc * tq, tq)
    sw = _dot(kwn_ref[pl.ds(w0, nw * tq), :], qt)
    gap_w = q0 + _iota((nw * tq, tq), 1) - (w0 + _iota((nw * tq, tq), 0))
    mask_w = (gap_w >= 0) & (gap_w < WINDOW)
    o_win = []
    for r in range(r_):
        p, den = _col_softmax(sw[:, head(r)], mask_w, may_be_empty=False)
        p = p.astype(BF16)
        o = _dot(vwt_ref[wc], p[0:tq, :])
        for c in range(1, nw):
            o = o + _dot(vwt_ref[wc + c], p[c * tq:(c + 1) * tq, :])
        o_win.append(o / den)

    gate = jax.nn.sigmoid(gl_ref[...].astype(F32)).T
    for r in range(r_):
        c = N_BRANCH * r
        o_slc = acc_ref[:, head(r)] / l_ref[:, head(r)]
        o = (gate[c:c + 1, :] * o_cmp[r] + gate[c + 1:c + 2, :] * o_slc
             + gate[c + 2:c + 3, :] * o_win[r])
        o_ref[:, r * d:(r + 1) * d] = o.T.astype(o_ref.dtype)


def nsa_attention(p, kc, vct, q_gain, k_gain, *, batch, seq, col, tq, tk):
    b, g, nc, d = kc.shape
    r_ = NSA_GROUP
    nq = seq // tq
    n_slc = seq // SLC_BLOCK
    assert seq % tq == 0 and seq % tk == 0 and WINDOW % tq == 0 and WINDOW + tq <= seq
    assert tq == tk and nq % 2 == 0 and n_slc <= d
    cols = r_ * tq
    cmp_start = jnp.arange(nc)[None, :] * CMP_STRIDE
    slc_start = jnp.arange(n_slc)[:, None] * SLC_BLOCK
    overlap = ((cmp_start < slc_start + SLC_BLOCK) & (cmp_start + CMP_BLOCK > slc_start)).astype(BF16)
    kv_spec = lambda blk: pl.BlockSpec((seq, d), lambda bi, gi, qi: (bi, blk + gi))
    return pl.pallas_call(
        functools.partial(_nsa_body, tq=tq, tk=tk, n_slc=n_slc, n_top=min(N_SLC, n_slc)),
        grid=(batch, g, nq),
        in_specs=[pl.BlockSpec((tq, r_ * d), lambda bi, gi, qi: (bi * nq + qi, col["q"] // r_ + gi)),
                  pl.BlockSpec((None, None, nc, d), lambda bi, gi, qi: (bi, gi, 0, 0)),
                  pl.BlockSpec((None, None, d, nc), lambda bi, gi, qi: (bi, gi, 0, 0)),
                  _resident((n_slc, nc), lambda bi, gi, qi: (0, 0)),
                  kv_spec(col["ks"]), kv_spec(col["vs"]), kv_spec(col["kw"]), kv_spec(col["vw"]),
                  pl.BlockSpec((tq, LANES), lambda bi, gi, qi: (bi * nq + qi, col["gate"] + gi)),
                  pl.BlockSpec((1, d), lambda bi, gi, qi: (0, 0)),
                  pl.BlockSpec((N_BRANCH, d), lambda bi, gi, qi: (0, 0))],
        out_specs=pl.BlockSpec((tq, r_ * d), lambda bi, gi, qi: (bi * nq + qi, gi)),
        out_shape=jax.ShapeDtypeStruct((batch * seq, g * r_ * d), BF16),
        scratch_shapes=[pltpu.VMEM((seq, 2 * d), BF16), pltpu.VMEM((seq, d), BF16),
                        pltpu.VMEM((seq // tk, d, tk), BF16), pltpu.VMEM((seq // tq, d, tq), BF16),
                        pltpu.VMEM((1, cols), F32), pltpu.VMEM((1, cols), F32),
                        pltpu.VMEM((d, cols), F32)],
        compiler_params=_params("parallel", "parallel", "arbitrary"),
        name="nsa_attention",
    )(p, kc, vct, overlap, p, p, p, p, p, q_gain.reshape(1, d), k_gain)


def _pad_cols(w, n):
    return jnp.pad(w, ((0, 0), (0, n - w.shape[1])))


def _even_mixer(x, batch, seq, norm_g, w_in, q_gain, k_gain, cmp_pe, kw1, kw2, vw1, vw2,
                gmlp_norm, gmlp_ws, gmlp_b):
    d = HEAD_DIM
    dm = x.shape[1]
    gmlp_w = gmlp_ws.shape[0] * GMLP_GROUP_DIM
    qw = dm - gmlp_w
    g_kv = max(1, qw // d // NSA_GROUP)
    kv = g_kv * d
    n_kv = 2 * N_BRANCH
    n_gate = N_BRANCH * qw // d
    assert w_in.shape[1] == qw + n_kv * kv + n_gate + 2 * gmlp_w
    o_gl = qw + n_kv * kv
    o_u = o_gl + n_gate
    per_g = n_gate // g_kv
    gate_cols = [_pad_cols(w_in[:, o_gl + gi * per_g: o_gl + (gi + 1) * per_g], LANES) for gi in range(g_kv)]
    w_nsa = jnp.concatenate([w_in[:, :o_gl]] + gate_cols, axis=1)
    u_blk = -(-w_nsa.shape[1] // gmlp_w)
    w_cat = jnp.concatenate([_pad_cols(w_nsa, u_blk * gmlp_w), w_in[:, o_u:]], axis=1).astype(BF16)
    tn = 512
    assert w_cat.shape[1] % tn == 0
    col = {"q": 0, "kc": qw // d, "vc": (qw + kv) // d, "ks": (qw + 2 * kv) // d, "vs": (qw + 3 * kv) // d,
           "kw": (qw + 4 * kv) // d, "vw": (qw + 5 * kv) // d, "gate": o_gl // d}

    p = norm_matmul(x, norm_g, w_cat, tm=1024, tn=tn)

    hd = CMP_STRIDE * d
    kc, vct = compress(p, cmp_pe, kw1.reshape(2, hd, d).astype(BF16), kw2.astype(BF16),
                       vw1.reshape(2, hd, d).astype(BF16), vw2.astype(BF16), k_gain[0:1],
                       batch=batch, seq=seq, g_kv=g_kv, k_blk=col["kc"], v_blk=col["vc"])
    oa = nsa_attention(p, kc, vct, q_gain, k_gain, batch=batch, seq=seq, col=col, tq=256, tk=256)
    ob = gmlp(p, gmlp_norm, gmlp_ws, gmlp_b, u_blk=u_blk, v_blk=u_blk + 1, tt=512)
    return oa, ob


def kernel(x, ev_norm, ev_w_in, ev_q_gain, ev_k_gain, ev_cmp_pe, ev_cmp_k_w1, ev_cmp_k_w2, ev_cmp_v_w1, ev_cmp_v_w2, ev_gmlp_norm, ev_gmlp_ws, ev_gmlp_b, ev_w_out, od_norm, od_w_in, od_conv_w, od_w_out, ffn_norm, ffn_w_in, ffn_conv_w, ffn_conv_b, ffn_w_down):
    batch, seq, dm = x.shape
    depth = ffn_norm.shape[0]
    ev_w_out, od_w_in, od_w_out, ffn_w_in, ffn_w_down = (
        w.astype(BF16) for w in (ev_w_out, od_w_in, od_w_out, ffn_w_in, ffn_w_down))
    no_bias = jnp.zeros((od_conv_w.shape[0], od_conv_w.shape[2]), F32)
    in_tiles = dict(seq=seq, tm=1024, tn=512, rc=256)
    h = x.reshape(batch * seq, dm)
    for i in range(depth):
        j = i // 2
        if i % 2 == 0:
            oa, ob = _even_mixer(h, batch, seq, ev_norm[j], ev_w_in[j], ev_q_gain[j], ev_k_gain[j],
                                 ev_cmp_pe[j], ev_cmp_k_w1[j], ev_cmp_k_w2[j], ev_cmp_v_w1[j], ev_cmp_v_w2[j],
                                 ev_gmlp_norm[j], ev_gmlp_ws[j], ev_gmlp_b[j])
            h = matmul_residual(h, [oa, ob], ev_w_out, layer=j, tm=256)
        else:
            m = gated_in_proj(h, od_norm, od_w_in, od_conv_w, no_bias, layer=j, mode="sconv", **in_tiles)
            h = matmul_residual(h, [m], od_w_out, layer=j, tm=256)
        f = gated_in_proj(h, ffn_norm, ffn_w_in, ffn_conv_w, ffn_conv_b, layer=i, mode="ffn", **in_tiles)
        h = matmul_residual(h, [f], ffn_w_down, layer=i, tm=256)
    return h.reshape(batch, seq, dm)
```

```python
import functools

import jax
import jax.numpy as jnp
from jax import lax
from jax.experimental import pallas as pl
from jax.experimental.pallas import tpu as pltpu

F32 = jnp.float32
BF16 = jnp.bfloat16

HEAD_DIM = 128
NSA_GROUP = 4
N_BRANCH = 3
CMP_BLOCK = 32
CMP_STRIDE = 16
SLC_BLOCK = 64
SLC_SHIFT = SLC_BLOCK.bit_length() - 1
LOG2_E = 1.4426950408889634
QK_AHEAD = 2
N_SLC = 16
N_LOCAL_SLC = 2
WINDOW = 512
FORCE_SCORE = 1e9
GMLP_GROUP_DIM = 128
GMLP_CHUNK = 128
CONV_WIDTH = 3
EPS = 1e-6
NEG = -1e30

LANES = 128
F32_SUBLANES = 8
VMEM_LIMIT = 56 * 1024 * 1024


def _params(*sem):
    return pltpu.CompilerParams(dimension_semantics=sem, vmem_limit_bytes=VMEM_LIMIT)


def _dot(a, b):
    return jnp.dot(a, b, preferred_element_type=F32)


def _rms(x, gain):
    return x * lax.rsqrt(jnp.mean(x * x, axis=-1, keepdims=True) + EPS) * gain


def _iota(shape, axis):
    return lax.broadcasted_iota(jnp.int32, shape, axis)


def _resident(shape, index_map):
    return pl.BlockSpec(shape, index_map, pipeline_mode=pl.Buffered(1))


def _normed_rows(x_ref, g_ref, xn_ref, rows, first):
    if not first:
        return xn_ref[rows, :]
    xc = _rms(x_ref[rows, :], g_ref[...]).astype(BF16)
    xn_ref[rows, :] = xc
    return xc


def _first_then_rest(j, body):
    pl.when(j == 0)(functools.partial(body, True))
    pl.when(j != 0)(functools.partial(body, False))


def _norm_matmul_body(x_ref, g_ref, w_ref, o_ref, xn_ref, *, rc):
    def body(first):
        for c in range(x_ref.shape[0] // rc):
            rows = slice(c * rc, (c + 1) * rc)
            xc = _normed_rows(x_ref, g_ref, xn_ref, rows, first)
            o_ref[rows, :] = _dot(xc, w_ref[...]).astype(o_ref.dtype)

    _first_then_rest(pl.program_id(1), body)


def norm_matmul(x, g, w, *, tm, tn, rc):
    t, d = x.shape
    n = w.shape[1]
    assert t % tm == 0 and n % tn == 0 and tm % rc == 0
    return pl.pallas_call(
        functools.partial(_norm_matmul_body, rc=rc),
        grid=(t // tm, n // tn),
        in_specs=[pl.BlockSpec((tm, d), lambda i, j: (i, 0)),
                  _resident((1, d), lambda i, j: (0, 0)),
                  pl.BlockSpec((d, tn), lambda i, j: (0, j))],
        out_specs=pl.BlockSpec((tm, tn), lambda i, j: (i, j)),
        out_shape=jax.ShapeDtypeStruct((t, n), BF16),
        scratch_shapes=[pltpu.VMEM((tm, d), BF16)],
        compiler_params=_params("parallel", "arbitrary"),
        name="norm_matmul",
    )(x, g.reshape(1, d), w)


def _shift_rows(cur, tail, k):
    out = pltpu.roll(cur, k, axis=0)
    row = _iota(cur.shape, 0)
    for r in range(k):
        src = F32_SUBLANES - k + r
        out = jnp.where(row == r, tail[src:src + 1, :], out)
    return out


def _conv3(cur, tail, cw):
    return (cw[0:1, :] * _shift_rows(cur, tail, 2) + cw[1:2, :] * _shift_rows(cur, tail, 1)
            + cw[2:3, :] * cur)


def _gated_in_body(x_ref, gn_ref, *rest, mode, rc, seq_tiles):
    n_parts = 2 if mode == "ffn" else 3
    w_refs = rest[:n_parts]
    cw_ref, cb_ref, h_ref, xn_ref, tail_ref = rest[n_parts:]
    i, j = pl.program_id(0), pl.program_id(1)
    tm = x_ref.shape[0]

    @pl.when(i % seq_tiles == 0)
    def _():
        tail_ref[j] = jnp.zeros(tail_ref.shape[1:], F32)

    def body(first):
        cw = cw_ref[...]
        tail = tail_ref[j]
        for c in range(tm // rc):
            rows = slice(c * rc, (c + 1) * rc)
            xc = _normed_rows(x_ref, gn_ref, xn_ref, rows, first)
            parts = [_dot(xc, w_ref[...]) for w_ref in w_refs]
            if mode == "ffn":
                g, u = parts
                h = jax.nn.silu(_conv3(g, tail, cw) + cb_ref[...]) * u
                tail = g[rc - F32_SUBLANES:, :]
            else:
                a, cc, dd = parts
                m = cc * dd
                h = a * _conv3(m, tail, cw)
                tail = m[rc - F32_SUBLANES:, :]
            h_ref[rows, :] = h.astype(h_ref.dtype)
        tail_ref[j] = tail

    _first_then_rest(j, body)


def gated_in_proj(x, g, w, conv_w, conv_b, *, layer, mode, seq, tm, tn, rc):
    t, d = x.shape
    n_parts = 2 if mode == "ffn" else 3
    f = w.shape[2] // n_parts
    assert t % tm == 0 and f % tn == 0 and seq % tm == 0 and tm % rc == 0
    nj = f // tn
    w_specs = [pl.BlockSpec((None, d, tn), functools.partial(lambda i, j, p: (layer, 0, p * nj + j), p=p))
               for p in range(n_parts)]
    return pl.pallas_call(
        functools.partial(_gated_in_body, mode=mode, rc=rc, seq_tiles=seq // tm),
        grid=(t // tm, nj),
        in_specs=[pl.BlockSpec((tm, d), lambda i, j: (i, 0)),
                  _resident((None, 1, d), lambda i, j: (layer, 0, 0))] + w_specs + [
                  pl.BlockSpec((None, CONV_WIDTH, tn), lambda i, j: (layer, 0, j)),
                  pl.BlockSpec((None, 1, tn), lambda i, j: (layer, 0, j))],
        out_specs=pl.BlockSpec((tm, tn), lambda i, j: (i, j)),
        out_shape=jax.ShapeDtypeStruct((t, f), BF16),
        scratch_shapes=[pltpu.VMEM((tm, d), BF16), pltpu.VMEM((nj, F32_SUBLANES, tn), F32)],
        compiler_params=_params("arbitrary", "arbitrary"),
        name=mode + "_in_proj",
    )(x, g[:, None, :], *([w] * n_parts), conv_w, conv_b[:, None, :])


def _matmul_residual_body(x_ref, *rest):
    n = (len(rest) - 1) // 2
    h_refs, w_refs, o_ref = rest[:n], rest[n:2 * n], rest[2 * n]
    acc = x_ref[...]
    for h_ref, w_ref in zip(h_refs, w_refs):
        acc = acc + _dot(h_ref[...], w_ref[...])
    o_ref[...] = acc


def matmul_residual(x, hs, w, *, layer, tm):
    t, d = x.shape
    k = hs[0].shape[1]
    assert all(h.shape == (t, k) for h in hs) and w.shape[1:] == (k * len(hs), d) and t % tm == 0
    h_specs = [pl.BlockSpec((tm, k), lambda i: (i, 0)) for _ in hs]
    w_specs = [_resident((None, k, d), functools.partial(lambda i, p: (layer, p, 0), p=p))
               for p in range(len(hs))]
    return pl.pallas_call(
        _matmul_residual_body,
        grid=(t // tm,),
        in_specs=[pl.BlockSpec((tm, d), lambda i: (i, 0))] + h_specs + w_specs,
        out_specs=pl.BlockSpec((tm, d), lambda i: (i, 0)),
        out_shape=jax.ShapeDtypeStruct((t, d), F32),
        compiler_params=_params("parallel"),
        name="out_proj",
    )(x, *hs, *([w] * len(hs)))


def _gmlp_body(u_ref, v_ref, gain_ref, ws_ref, bs_ref, o_ref, *, chunks):
    c, gd = GMLP_CHUNK, GMLP_GROUP_DIM
    causal = _iota((c, c), 1) <= _iota((c, c), 0)
    for g in range(ws_ref.shape[0]):
        cols = slice(g * gd, (g + 1) * gd)
        v = jax.nn.gelu(v_ref[:, cols].astype(F32))
        vc = v - jnp.mean(v, axis=-1, keepdims=True)
        vn = vc * lax.rsqrt(jnp.mean(vc * vc, axis=-1, keepdims=True) + EPS) * gain_ref[g]
        vn = vn.astype(BF16)
        w = jnp.where(causal, ws_ref[g], 0.0).astype(BF16)
        bias = bs_ref[g]
        for ci in range(chunks):
            rows = slice(ci * c, (ci + 1) * c)
            mixed = _dot(w, vn[rows, :]) + bias
            u = jax.nn.gelu(u_ref[rows, cols].astype(F32))
            o_ref[rows, cols] = (u * mixed).astype(o_ref.dtype)


def gmlp(p, gain, ws, bs, *, u_blk, v_blk, tt):
    t = p.shape[0]
    groups, c, _ = ws.shape
    gd = GMLP_GROUP_DIM
    width = groups * gd
    assert t % tt == 0 and tt % c == 0
    full = lambda shape: _resident(shape, lambda i: (0,) * len(shape))
    return pl.pallas_call(
        functools.partial(_gmlp_body, chunks=tt // c),
        grid=(t // tt,),
        in_specs=[pl.BlockSpec((tt, width), lambda i: (i, u_blk)),
                  pl.BlockSpec((tt, width), lambda i: (i, v_blk)),
                  full((groups, 1, gd)), full((groups, c, c)), full((groups, c, 1))],
        out_specs=pl.BlockSpec((tt, width), lambda i: (i, 0)),
        out_shape=jax.ShapeDtypeStruct((t, width), BF16),
        compiler_params=_params("parallel"),
        name="gmlp",
    )(p, p, gain.reshape(groups, 1, gd), ws, bs.reshape(groups, c, 1))


def _compress_body(k_ref, v_ref, pe_ref, kw1_ref, kw2_ref, vw1_ref, vw2_ref, kg_ref, kc_ref, vct_ref, x_ref):
    st = CMP_STRIDE
    n = x_ref.shape[0] // st

    def mlp(src_ref, w1_ref, w2_ref):
        x_ref[...] = src_ref[...].astype(F32)

        def half(part):
            return jnp.concatenate(
                [(x_ref[pl.ds(l, n, stride=st), :] + pe_ref[part * st + l:part * st + l + 1, :]).astype(BF16)
                 for l in range(st)], axis=1)

        a = _dot(half(0), w1_ref[0])
        b = _dot(half(1), w1_ref[1])
        hid = jax.nn.gelu(a + pltpu.roll(b, n - 1, axis=0))
        return _dot(hid.astype(BF16), w2_ref[...])

    kc_ref[...] = _rms(mlp(k_ref, kw1_ref, kw2_ref), kg_ref[...]).astype(kc_ref.dtype)
    vct_ref[...] = mlp(v_ref, vw1_ref, vw2_ref).T.astype(vct_ref.dtype)


def compress(p, pe, kw1, kw2, vw1, vw2, kgain, *, batch, seq, g_kv, k_blk, v_blk):
    d = HEAD_DIM
    st = CMP_STRIDE
    n = seq // st
    hd = st * d
    assert CMP_BLOCK == 2 * st and pe.shape == (CMP_BLOCK, d) and kw1.shape == (2, hd, d)
    full = lambda shape: pl.BlockSpec(shape, lambda i, j: (0,) * len(shape))
    return pl.pallas_call(
        _compress_body,
        grid=(batch, g_kv),
        in_specs=[pl.BlockSpec((seq, d), lambda i, j: (i, k_blk + j)),
                  pl.BlockSpec((seq, d), lambda i, j: (i, v_blk + j)),
                  full((CMP_BLOCK, d)), full((2, hd, d)), full((d, d)),
                  full((2, hd, d)), full((d, d)), full((1, d))],
        out_specs=[pl.BlockSpec((None, None, n, d), lambda i, j: (i, j, 0, 0)),
                   pl.BlockSpec((None, None, d, n), lambda i, j: (i, j, 0, 0))],
        out_shape=[jax.ShapeDtypeStruct((batch, g_kv, n, d), BF16),
                   jax.ShapeDtypeStruct((batch, g_kv, d, n), BF16)],
        scratch_shapes=[pltpu.VMEM((seq, d), F32)],
        compiler_params=_params("parallel", "arbitrary"),
        name="nsa_compress",
    )(p, p, pe, kw1, kw2, vw1, vw2, kgain)


def _col_softmax(s, mask, *, may_be_empty):
    s = jnp.where(mask, s, NEG)
    p = jnp.exp2(s - jnp.max(s, axis=0, keepdims=True))
    if may_be_empty:
        p = jnp.where(mask, p, 0.0)
    return p, jnp.sum(p, axis=0, keepdims=True)


def _nsa_body(q_ref, kc_ref, vct_ref, ov_ref, kv_ref, gl_ref, qg_ref, kg_ref, o_ref,
              ksx_ref, kwn_ref, vst_ref, vwt_ref, m_ref, l_ref, acc_ref, *, tq, tk, n_slc, n_top):
    r_, d = NSA_GROUP, HEAD_DIM
    cols = r_ * tq
    nc = kc_ref.shape[0]
    seq = kv_ref.shape[0]
    g_kv = kv_ref.shape[1] // (4 * d)
    qi = pl.program_id(2)
    q0 = qi * tq
    head = lambda r: slice(r * tq, (r + 1) * tq)

    def stage_kv(g):
        sect = lambda n: slice((n * g_kv + g) * d, (n * g_kv + g + 1) * d)
        ksx_ref[:, 0:d] = _rms(kv_ref[:, sect(0)].astype(F32), kg_ref[1:2, :]).astype(BF16)
        ksx_ref[:, d:2 * d] = (_iota((seq, d), 0) >> SLC_SHIFT == _iota((seq, d), 1)).astype(BF16)
        kwn_ref[...] = _rms(kv_ref[:, sect(2)].astype(F32), kg_ref[2:3, :]).astype(BF16)
        for c in range(seq // tk):
            vst_ref[c] = kv_ref[c * tk:(c + 1) * tk, sect(1)].astype(F32).T.astype(BF16)
        for c in range(seq // tq):
            vwt_ref[c] = kv_ref[c * tq:(c + 1) * tq, sect(3)].astype(F32).T.astype(BF16)

    for g in range(g_kv):
        pl.when((qi == 0) & (pl.program_id(1) == g))(functools.partial(stage_kv, g))

    qraw = q_ref[...].astype(F32)
    qt = jnp.concatenate(
        [(_rms(qraw[:, r * d:(r + 1) * d], qg_ref[...]) * (d ** -0.5 * LOG2_E)).T for r in range(r_)],
        axis=1).astype(BF16)

    sc = _dot(kc_ref[...], qt)
    mask_c = _iota((nc, tq), 0) * CMP_STRIDE + (CMP_BLOCK - 1) <= q0 + _iota((nc, tq), 1)
    o_cmp = []
    psum = None
    for r in range(r_):
        p, den = _col_softmax(sc[:, head(r)], mask_c, may_be_empty=True)
        p = p / jnp.maximum(den, 1e-30)
        psum = p if psum is None else psum + p
        o_cmp.append(_dot(vct_ref[...], p.astype(BF16)))

    overlap = ov_ref[...]
    hi = psum.astype(BF16)
    r1 = psum - hi.astype(F32)
    lo = r1.astype(BF16)
    lo2 = (r1 - lo.astype(F32)).astype(BF16)
    imp = _dot(overlap, hi) + _dot(overlap, lo) + _dot(overlap, lo2)

    j_s = _iota((n_slc, tq), 0)
    pos_s = q0 + _iota((n_slc, tq), 1)
    dlt = (pos_s >> SLC_SHIFT) - j_s
    forced = (j_s == 0) | ((dlt >= 0) & (dlt < N_LOCAL_SLC))
    val = jnp.where(dlt >= 0, imp, NEG)
    val = jnp.where(forced, FORCE_SCORE, val)
    sub = F32_SUBLANES
    groups = [val[g0:g0 + sub, :] for g0 in range(0, n_slc, sub)]
    cnts = [jnp.zeros((sub, tq), jnp.int32) for _ in groups]
    j_g = _iota((sub, tq), 0)
    for i in range(n_slc):
        vi = val[i:i + 1, :]
        for gi, vg in enumerate(groups):
            g0 = gi * sub
            if g0 > i:
                beats = vi >= vg
            elif g0 + sub - 1 < i:
                beats = vi > vg
            else:
                beats = (vi > vg) | ((vi == vg) & (j_g + g0 > i))
            cnts[gi] = cnts[gi] + beats.astype(jnp.int32)
    cnt = jnp.concatenate(cnts, axis=0)
    bias = jnp.where(cnt < n_top, 0.0, NEG)
    if n_slc < d:
        bias = jnp.concatenate([bias, jnp.zeros((d - n_slc, tq), F32)], axis=0)
    qx = jnp.concatenate([qt, jnp.concatenate([bias.astype(BF16)] * r_, axis=1)], axis=0)

    m_ref[...] = jnp.full((1, cols), NEG, F32)
    l_ref[...] = jnp.zeros((1, cols), F32)
    acc_ref[...] = jnp.zeros((d, cols), F32)

    def scores(kt):
        k0 = pl.multiple_of(kt * tk, tk)
        return _dot(ksx_ref[pl.ds(k0, tk), :], qx)

    def fold(kt, r, s, causal):
        if causal is not None:
            s = jnp.where(causal >= kt * tk, s, NEG)
        m_old = m_ref[:, head(r)]
        m_new = jnp.maximum(m_old, jnp.max(s, axis=0, keepdims=True))
        alpha = jnp.exp2(m_old - m_new)
        p = jnp.exp2(s - m_new)
        l_ref[:, head(r)] = alpha * l_ref[:, head(r)] + jnp.sum(p, axis=0, keepdims=True)
        acc_ref[:, head(r)] = alpha * acc_ref[:, head(r)] + _dot(vst_ref[kt], p.astype(BF16))
        m_ref[:, head(r)] = m_new

    def slc_tiles(kts, causal):
        s_all = [scores(kt) for kt in kts]
        for kt, s in zip(kts, s_all):
            for r in range(r_):
                fold(kt, r, s[:, head(r)], causal)

    n_quads = q0 // (4 * tk)

    def quad(i, carry):
        slc_tiles([4 * i + dd for dd in range(4)], None)
        return carry

    lax.fori_loop(0, n_quads, quad, 0)
    causal_gap = q0 + _iota((tk, tq), 1) - _iota((tk, tq), 0)
    n_tiles = (q0 + tq + tk - 1) // tk

    def causal_pair(i, carry):
        slc_tiles([4 * n_quads + 2 * i, 4 * n_quads + 2 * i + 1], causal_gap)
        return carry

    lax.fori_loop(0, (n_tiles - 4 * n_quads + 1) // 2, causal_pair, 0)

    nw = WINDOW // tq + 1
    wc = jnp.maximum(qi - WINDOW // tq, 0)
    w0 = pl.multiple_of(wc * tq, tq)
    sw = _dot(kwn_ref[pl.ds(w0, nw * tq), :], qt)
    gap_w = q0 + _iota((nw * tq, tq), 1) - (w0 + _iota((nw * tq, tq), 0))
    mask_w = (gap_w >= 0) & (gap_w < WINDOW)
    o_win = []
    for r in range(r_):
        p, den = _col_softmax(sw[:, head(r)], mask_w, may_be_empty=False)
        p = p.astype(BF16)
        o = _dot(vwt_ref[wc], p[0:tq, :])
        for c in range(1, nw):
            o = o + _dot(vwt_ref[wc + c], p[c * tq:(c + 1) * tq, :])
        o_win.append(o / den)

    gate = jax.nn.sigmoid(gl_ref[...].astype(F32)).T
    for r in range(r_):
        c = N_BRANCH * r
        o_slc = acc_ref[:, head(r)] / l_ref[:, head(r)]
        o = (gate[c:c + 1, :] * o_cmp[r] + gate[c + 1:c + 2, :] * o_slc
             + gate[c + 2:c + 3, :] * o_win[r])
        o_ref[:, r * d:(r + 1) * d] = o.T.astype(o_ref.dtype)


def nsa_attention(p, kc, vct, q_gain, k_gain, *, batch, seq, col, tq, tk):
    b, g, nc, d = kc.shape
    r_ = NSA_GROUP
    nq = seq // tq
    n_slc = seq // SLC_BLOCK
    assert seq % tq == 0 and seq % tk == 0 and WINDOW % tq == 0 and WINDOW + tq <= seq
    assert tq == tk and nq % 2 == 0 and n_slc <= d
    cols = r_ * tq
    cmp_start = jnp.arange(nc)[None, :] * CMP_STRIDE
    slc_start = jnp.arange(n_slc)[:, None] * SLC_BLOCK
    overlap = ((cmp_start < slc_start + SLC_BLOCK) & (cmp_start + CMP_BLOCK > slc_start)).astype(BF16)
    kv4 = 4 * g * d
    return pl.pallas_call(
        functools.partial(_nsa_body, tq=tq, tk=tk, n_slc=n_slc, n_top=min(N_SLC, n_slc)),
        grid=(batch, g, nq),
        in_specs=[pl.BlockSpec((tq, r_ * d), lambda bi, gi, qi: (bi * nq + qi, col["q"] // r_ + gi)),
                  pl.BlockSpec((None, None, nc, d), lambda bi, gi, qi: (bi, gi, 0, 0)),
                  pl.BlockSpec((None, None, d, nc), lambda bi, gi, qi: (bi, gi, 0, 0)),
                  _resident((n_slc, nc), lambda bi, gi, qi: (0, 0)),
                  pl.BlockSpec((seq, kv4), lambda bi, gi, qi: (bi, col["kv4"])),
                  pl.BlockSpec((tq, LANES), lambda bi, gi, qi: (bi * nq + qi, col["gate"] + gi)),
                  pl.BlockSpec((1, d), lambda bi, gi, qi: (0, 0)),
                  pl.BlockSpec((N_BRANCH, d), lambda bi, gi, qi: (0, 0))],
        out_specs=pl.BlockSpec((tq, r_ * d), lambda bi, gi, qi: (bi * nq + qi, gi)),
        out_shape=jax.ShapeDtypeStruct((batch * seq, g * r_ * d), BF16),
        scratch_shapes=[pltpu.VMEM((seq, 2 * d), BF16), pltpu.VMEM((seq, d), BF16),
                        pltpu.VMEM((seq // tk, d, tk), BF16), pltpu.VMEM((seq // tq, d, tq), BF16),
                        pltpu.VMEM((1, cols), F32), pltpu.VMEM((1, cols), F32),
                        pltpu.VMEM((d, cols), F32)],
        compiler_params=_params("parallel", "arbitrary", "arbitrary"),
        name="nsa_attention",
    )(p, kc, vct, overlap, p, p, q_gain.reshape(1, d), k_gain)


def _pad_cols(w, n):
    return jnp.pad(w, ((0, 0), (0, n - w.shape[1])))


def _even_mixer(x, batch, seq, norm_g, w_in, q_gain, k_gain, cmp_pe, kw1, kw2, vw1, vw2,
                gmlp_norm, gmlp_ws, gmlp_b):
    d = HEAD_DIM
    dm = x.shape[1]
    gmlp_w = gmlp_ws.shape[0] * GMLP_GROUP_DIM
    qw = dm - gmlp_w
    g_kv = max(1, qw // d // NSA_GROUP)
    kv = g_kv * d
    n_kv = 2 * N_BRANCH
    n_gate = N_BRANCH * qw // d
    assert w_in.shape[1] == qw + n_kv * kv + n_gate + 2 * gmlp_w
    o_gl = qw + n_kv * kv
    o_u = o_gl + n_gate
    per_g = n_gate // g_kv
    gate_cols = [_pad_cols(w_in[:, o_gl + gi * per_g: o_gl + (gi + 1) * per_g], LANES) for gi in range(g_kv)]
    w_nsa = jnp.concatenate([w_in[:, :qw], w_in[:, qw + 2 * kv:o_gl], w_in[:, qw:qw + 2 * kv]] + gate_cols, axis=1)
    u_blk = -(-w_nsa.shape[1] // gmlp_w)
    w_cat = jnp.concatenate([_pad_cols(w_nsa, u_blk * gmlp_w), w_in[:, o_u:]], axis=1).astype(BF16)
    tn = 512
    assert w_cat.shape[1] % tn == 0 and qw % (4 * kv) == 0
    col = {"q": 0, "kv4": qw // (4 * kv), "kc": (qw + 4 * kv) // d, "vc": (qw + 5 * kv) // d, "gate": o_gl // d}

    p = norm_matmul(x, norm_g, w_cat, tm=1024, tn=tn, rc=256)

    hd = CMP_STRIDE * d
    kc, vct = compress(p, cmp_pe, kw1.reshape(2, hd, d).astype(BF16), kw2.astype(BF16),
                       vw1.reshape(2, hd, d).astype(BF16), vw2.astype(BF16), k_gain[0:1],
                       batch=batch, seq=seq, g_kv=g_kv, k_blk=col["kc"], v_blk=col["vc"])
    oa = nsa_attention(p, kc, vct, q_gain, k_gain, batch=batch, seq=seq, col=col, tq=256, tk=256)
    ob = gmlp(p, gmlp_norm, gmlp_ws, gmlp_b, u_blk=u_blk, v_blk=u_blk + 1, tt=512)
    return oa, ob


def kernel(x, ev_norm, ev_w_in, ev_q_gain, ev_k_gain, ev_cmp_pe, ev_cmp_k_w1, ev_cmp_k_w2, ev_cmp_v_w1, ev_cmp_v_w2, ev_gmlp_norm, ev_gmlp_ws, ev_gmlp_b, ev_w_out, od_norm, od_w_in, od_conv_w, od_w_out, ffn_norm, ffn_w_in, ffn_conv_w, ffn_conv_b, ffn_w_down):
    batch, seq, dm = x.shape
    depth = ffn_norm.shape[0]
    ev_w_out, od_w_in, od_w_out, ffn_w_in, ffn_w_down = (
        w.astype(BF16) for w in (ev_w_out, od_w_in, od_w_out, ffn_w_in, ffn_w_down))
    no_bias = jnp.zeros((od_conv_w.shape[0], od_conv_w.shape[2]), F32)
    in_tiles = dict(seq=seq, tm=1024, tn=512, rc=256)
    h = x.reshape(batch * seq, dm)
    for i in range(depth):
        j = i // 2
        if i % 2 == 0:
            oa, ob = _even_mixer(h, batch, seq, ev_norm[j], ev_w_in[j], ev_q_gain[j], ev_k_gain[j],
                                 ev_cmp_pe[j], ev_cmp_k_w1[j], ev_cmp_k_w2[j], ev_cmp_v_w1[j], ev_cmp_v_w2[j],
                                 ev_gmlp_norm[j], ev_gmlp_ws[j], ev_gmlp_b[j])
            h = matmul_residual(h, [oa, ob], ev_w_out, layer=j, tm=512)
        else:
            m = gated_in_proj(h, od_norm, od_w_in, od_conv_w, no_bias, layer=j, mode="sconv", **in_tiles)
            h = matmul_residual(h, [m], od_w_out, layer=j, tm=512)
        f = gated_in_proj(h, ffn_norm, ffn_w_in, ffn_conv_w, ffn_conv_b, layer=i, mode="ffn", **in_tiles)
        h = matmul_residual(h, [f], ffn_w_down, layer=i, tm=256)
    return h.reshape(batch, seq, dm)
```

```python
import functools

import jax
import jax.numpy as jnp
from jax import lax
from jax.experimental import pallas as pl
from jax.experimental.pallas import tpu as pltpu

F32 = jnp.float32
BF16 = jnp.bfloat16

HEAD_DIM = 128
NSA_GROUP = 4
N_BRANCH = 3
CMP_BLOCK = 32
CMP_STRIDE = 16
SLC_BLOCK = 64
SLC_SHIFT = SLC_BLOCK.bit_length() - 1
LOG2_E = 1.4426950408889634
QK_AHEAD = 2
N_SLC = 16
N_LOCAL_SLC = 2
WINDOW = 512
FORCE_SCORE = 1e9
GMLP_GROUP_DIM = 128
GMLP_CHUNK = 128
CONV_WIDTH = 3
EPS = 1e-6
NEG = -1e30

LANES = 128
F32_SUBLANES = 8
BF16_SUBLANES = 16
VMEM_LIMIT = 56 * 1024 * 1024


def _params(*sem):
    return pltpu.CompilerParams(dimension_semantics=sem, vmem_limit_bytes=VMEM_LIMIT)


def _dot(a, b):
    return jnp.dot(a, b, preferred_element_type=F32)


def _rms(x, gain):
    return x * lax.rsqrt(jnp.mean(x * x, axis=-1, keepdims=True) + EPS) * gain


def _iota(shape, axis):
    return lax.broadcasted_iota(jnp.int32, shape, axis)


def _resident(shape, index_map):
    return pl.BlockSpec(shape, index_map, pipeline_mode=pl.Buffered(1))


def _cast_row_blocks(rows, n_steps):
    for nb in range(min(n_steps, rows // BF16_SUBLANES), 0, -1):
        if rows % nb == 0 and (rows // nb) % BF16_SUBLANES == 0:
            return nb
    raise ValueError(f"cannot split {rows} rows")


def _call_with_casts(body, *, grid, in_specs, out_spec, out_shape, scratch_shapes, sem, name, args, casts):
    n_steps = 1
    for n in grid:
        n_steps *= n

    def step(*idx):
        s = idx[0]
        for k, n in zip(idx[1:], grid[1:]):
            s = s * n + k
        return s

    n_in, n_cast = len(in_specs), len(casts)
    src_specs, dst_specs, dst_shapes = [], [], []
    for w, layer in casts:
        _, rows, c = w.shape
        nb = _cast_row_blocks(rows, n_steps)
        blk = functools.partial(lambda *idx, nb: jnp.minimum(step(*idx), nb - 1), nb=nb)
        src_specs.append(pl.BlockSpec((None, rows // nb, c),
                                      functools.partial(lambda *idx, blk, layer: (layer, blk(*idx), 0),
                                                        blk=blk, layer=layer)))
        dst_specs.append(pl.BlockSpec((None, rows // nb, c),
                                      functools.partial(lambda *idx, blk: (0, blk(*idx), 0), blk=blk)))
        dst_shapes.append(jax.ShapeDtypeStruct((1, rows, c), BF16))

    def with_casts(*refs):
        ins, src, rest = refs[:n_in], refs[n_in:n_in + n_cast], refs[n_in + n_cast:]
        out, dst, scratch = rest[0], rest[1:1 + n_cast], rest[1 + n_cast:]
        for s_ref, d_ref in zip(src, dst):
            d_ref[...] = s_ref[...].astype(BF16)
        body(*ins, out, *scratch)

    outs = pl.pallas_call(
        with_casts,
        grid=grid,
        in_specs=list(in_specs) + src_specs,
        out_specs=[out_spec] + dst_specs,
        out_shape=[out_shape] + dst_shapes,
        scratch_shapes=scratch_shapes,
        compiler_params=_params(*sem),
        name=name,
    )(*args, *[w for w, _ in casts])
    return outs[0], list(outs[1:])


def _normed_rows(x_ref, g_ref, xn_ref, rows, first):
    if not first:
        return xn_ref[rows, :]
    xc = _rms(x_ref[rows, :], g_ref[...]).astype(BF16)
    xn_ref[rows, :] = xc
    return xc


def _first_then_rest(j, body):
    pl.when(j == 0)(functools.partial(body, True))
    pl.when(j != 0)(functools.partial(body, False))


def _norm_matmul_body(x_ref, g_ref, w_ref, o_ref, xn_ref, *, rc):
    def body(first):
        for c in range(x_ref.shape[0] // rc):
            rows = slice(c * rc, (c + 1) * rc)
            xc = _normed_rows(x_ref, g_ref, xn_ref, rows, first)
            o_ref[rows, :] = _dot(xc, w_ref[...]).astype(o_ref.dtype)

    _first_then_rest(pl.program_id(1), body)


def norm_matmul(x, g, w, *, tm, tn, rc, casts=()):
    t, d = x.shape
    n = w.shape[1]
    assert t % tm == 0 and n % tn == 0 and tm % rc == 0
    return _call_with_casts(
        functools.partial(_norm_matmul_body, rc=rc),
        grid=(t // tm, n // tn),
        in_specs=[pl.BlockSpec((tm, d), lambda i, j: (i, 0)),
                  _resident((1, d), lambda i, j: (0, 0)),
                  pl.BlockSpec((d, tn), lambda i, j: (0, j))],
        out_spec=pl.BlockSpec((tm, tn), lambda i, j: (i, j)),
        out_shape=jax.ShapeDtypeStruct((t, n), BF16),
        scratch_shapes=[pltpu.VMEM((tm, d), BF16)],
        sem=("arbitrary", "arbitrary"),
        name="norm_matmul",
        args=(x, g.reshape(1, d), w),
        casts=casts)


def _shift_rows(cur, tail, k):
    out = pltpu.roll(cur, k, axis=0)
    row = _iota(cur.shape, 0)
    for r in range(k):
        src = F32_SUBLANES - k + r
        out = jnp.where(row == r, tail[src:src + 1, :], out)
    return out


def _conv3(cur, tail, cw):
    return (cw[0:1, :] * _shift_rows(cur, tail, 2) + cw[1:2, :] * _shift_rows(cur, tail, 1)
            + cw[2:3, :] * cur)


def _gated_in_body(x_ref, gn_ref, *rest, mode, rc, seq_tiles):
    n_parts = 2 if mode == "ffn" else 3
    w_refs = rest[:n_parts]
    cw_ref, cb_ref, h_ref, xn_ref, tail_ref = rest[n_parts:]
    i, j = pl.program_id(0), pl.program_id(1)
    tm = x_ref.shape[0]

    @pl.when(i % seq_tiles == 0)
    def _():
        tail_ref[j] = jnp.zeros(tail_ref.shape[1:], F32)

    def body(first):
        cw = cw_ref[...]
        tail = tail_ref[j]
        for c in range(tm // rc):
            rows = slice(c * rc, (c + 1) * rc)
            xc = _normed_rows(x_ref, gn_ref, xn_ref, rows, first)
            parts = [_dot(xc, w_ref[...]) for w_ref in w_refs]
            if mode == "ffn":
                g, u = parts
                h = jax.nn.silu(_conv3(g, tail, cw) + cb_ref[...]) * u
                tail = g[rc - F32_SUBLANES:, :]
            else:
                a, cc, dd = parts
                m = cc * dd
                h = a * _conv3(m, tail, cw)
                tail = m[rc - F32_SUBLANES:, :]
            h_ref[rows, :] = h.astype(h_ref.dtype)
        tail_ref[j] = tail

    _first_then_rest(j, body)


def gated_in_proj(x, g, w, w_layer, conv_w, conv_b, *, layer, mode, seq, tm, tn, rc, casts=()):
    t, d = x.shape
    n_parts = 2 if mode == "ffn" else 3
    f = w.shape[2] // n_parts
    assert t % tm == 0 and f % tn == 0 and seq % tm == 0 and tm % rc == 0
    nj = f // tn
    w_specs = [pl.BlockSpec((None, d, tn), functools.partial(lambda i, j, p: (w_layer, 0, p * nj + j), p=p))
               for p in range(n_parts)]
    return _call_with_casts(
        functools.partial(_gated_in_body, mode=mode, rc=rc, seq_tiles=seq // tm),
        grid=(t // tm, nj),
        in_specs=[pl.BlockSpec((tm, d), lambda i, j: (i, 0)),
                  _resident((None, 1, d), lambda i, j: (layer, 0, 0))] + w_specs + [
                  pl.BlockSpec((None, CONV_WIDTH, tn), lambda i, j: (layer, 0, j)),
                  pl.BlockSpec((None, 1, tn), lambda i, j: (layer, 0, j))],
        out_spec=pl.BlockSpec((tm, tn), lambda i, j: (i, j)),
        out_shape=jax.ShapeDtypeStruct((t, f), BF16),
        scratch_shapes=[pltpu.VMEM((tm, d), BF16), pltpu.VMEM((nj, F32_SUBLANES, tn), F32)],
        sem=("arbitrary", "arbitrary"),
        name=mode + "_in_proj",
        args=(x, g[:, None, :], *([w] * n_parts), conv_w, conv_b[:, None, :]),
        casts=casts)


def _matmul_residual_body(x_ref, *rest):
    n = (len(rest) - 1) // 2
    h_refs, w_refs, o_ref = rest[:n], rest[n:2 * n], rest[2 * n]
    acc = x_ref[...]
    for h_ref, w_ref in zip(h_refs, w_refs):
        acc = acc + _dot(h_ref[...], w_ref[...])
    o_ref[...] = acc


def matmul_residual(x, hs, w, *, tm, casts=()):
    t, d = x.shape
    k = hs[0].shape[1]
    assert all(h.shape == (t, k) for h in hs) and w.shape == (1, k * len(hs), d) and t % tm == 0
    h_specs = [pl.BlockSpec((tm, k), lambda i: (i, 0)) for _ in hs]
    w_specs = [_resident((None, k, d), functools.partial(lambda i, p: (0, p, 0), p=p))
               for p in range(len(hs))]
    return _call_with_casts(
        _matmul_residual_body,
        grid=(t // tm,),
        in_specs=[pl.BlockSpec((tm, d), lambda i: (i, 0))] + h_specs + w_specs,
        out_spec=pl.BlockSpec((tm, d), lambda i: (i, 0)),
        out_shape=jax.ShapeDtypeStruct((t, d), F32),
        scratch_shapes=[],
        sem=("arbitrary",),
        name="out_proj",
        args=(x, *hs, *([w] * len(hs))),
        casts=casts)


def _gmlp_body(u_ref, v_ref, gain_ref, ws_ref, bs_ref, o_ref, *, chunks):
    c, gd = GMLP_CHUNK, GMLP_GROUP_DIM
    causal = _iota((c, c), 1) <= _iota((c, c), 0)
    for g in range(ws_ref.shape[0]):
        cols = slice(g * gd, (g + 1) * gd)
        v = jax.nn.gelu(v_ref[:, cols].astype(F32))
        vc = v - jnp.mean(v, axis=-1, keepdims=True)
        vn = vc * lax.rsqrt(jnp.mean(vc * vc, axis=-1, keepdims=True) + EPS) * gain_ref[g]
        vn = vn.astype(BF16)
        w = jnp.where(causal, ws_ref[g], 0.0).astype(BF16)
        bias = bs_ref[g]
        for ci in range(chunks):
            rows = slice(ci * c, (ci + 1) * c)
            mixed = _dot(w, vn[rows, :]) + bias
            u = jax.nn.gelu(u_ref[rows, cols].astype(F32))
            o_ref[rows, cols] = (u * mixed).astype(o_ref.dtype)


def gmlp(p, gain, ws, bs, *, u_blk, v_blk, tt):
    t = p.shape[0]
    groups, c, _ = ws.shape
    gd = GMLP_GROUP_DIM
    width = groups * gd
    assert t % tt == 0 and tt % c == 0
    full = lambda shape: _resident(shape, lambda i: (0,) * len(shape))
    return pl.pallas_call(
        functools.partial(_gmlp_body, chunks=tt // c),
        grid=(t // tt,),
        in_specs=[pl.BlockSpec((tt, width), lambda i: (i, u_blk)),
                  pl.BlockSpec((tt, width), lambda i: (i, v_blk)),
                  full((groups, 1, gd)), full((groups, c, c)), full((groups, c, 1))],
        out_specs=pl.BlockSpec((tt, width), lambda i: (i, 0)),
        out_shape=jax.ShapeDtypeStruct((t, width), BF16),
        compiler_params=_params("parallel"),
        name="gmlp",
    )(p, p, gain.reshape(groups, 1, gd), ws, bs.reshape(groups, c, 1))


def _compress_body(k_ref, v_ref, pe_ref, kw1_ref, kw2_ref, vw1_ref, vw2_ref, kg_ref, kc_ref, vct_ref, x_ref):
    st = CMP_STRIDE
    n = x_ref.shape[0] // st

    def mlp(src_ref, w1_ref, w2_ref):
        x_ref[...] = src_ref[...].astype(F32)

        def half(part):
            return jnp.concatenate(
                [(x_ref[pl.ds(l, n, stride=st), :] + pe_ref[part * st + l:part * st + l + 1, :]).astype(BF16)
                 for l in range(st)], axis=1)

        a = _dot(half(0), w1_ref[0])
        b = _dot(half(1), w1_ref[1])
        hid = jax.nn.gelu(a + pltpu.roll(b, n - 1, axis=0))
        return _dot(hid.astype(BF16), w2_ref[...])

    kc_ref[...] = _rms(mlp(k_ref, kw1_ref, kw2_ref), kg_ref[...]).astype(kc_ref.dtype)
    vct_ref[...] = mlp(v_ref, vw1_ref, vw2_ref).T.astype(vct_ref.dtype)


def compress(p, pe, kw1, kw2, vw1, vw2, kgain, *, batch, seq, g_kv, k_blk, v_blk):
    d = HEAD_DIM
    st = CMP_STRIDE
    n = seq // st
    hd = st * d
    assert CMP_BLOCK == 2 * st and pe.shape == (CMP_BLOCK, d) and kw1.shape == (2, hd, d)
    full = lambda shape: pl.BlockSpec(shape, lambda i, j: (0,) * len(shape))
    return pl.pallas_call(
        _compress_body,
        grid=(batch, g_kv),
        in_specs=[pl.BlockSpec((seq, d), lambda i, j: (i, k_blk + j)),
                  pl.BlockSpec((seq, d), lambda i, j: (i, v_blk + j)),
                  full((CMP_BLOCK, d)), full((2, hd, d)), full((d, d)),
                  full((2, hd, d)), full((d, d)), full((1, d))],
        out_specs=[pl.BlockSpec((None, None, n, d), lambda i, j: (i, j, 0, 0)),
                   pl.BlockSpec((None, None, d, n), lambda i, j: (i, j, 0, 0))],
        out_shape=[jax.ShapeDtypeStruct((batch, g_kv, n, d), BF16),
                   jax.ShapeDtypeStruct((batch, g_kv, d, n), BF16)],
        scratch_shapes=[pltpu.VMEM((seq, d), F32)],
        compiler_params=_params("parallel", "arbitrary"),
        name="nsa_compress",
    )(p, p, pe, kw1, kw2, vw1, vw2, kgain)


def _col_softmax(s, mask, *, may_be_empty):
    s = jnp.where(mask, s, NEG)
    p = jnp.exp2(s - jnp.max(s, axis=0, keepdims=True))
    if may_be_empty:
        p = jnp.where(mask, p, 0.0)
    return p, jnp.sum(p, axis=0, keepdims=True)


def _nsa_body(q_ref, kc_ref, vct_ref, ov_ref, kv_ref, gl_ref, qg_ref, kg_ref, o_ref,
              ksx_ref, kwn_ref, vst_ref, vwt_ref, m_ref, l_ref, acc_ref, *, tq, tk, n_slc, n_top):
    r_, d = NSA_GROUP, HEAD_DIM
    cols = r_ * tq
    nc = kc_ref.shape[0]
    seq = kv_ref.shape[0]
    g_kv = kv_ref.shape[1] // (4 * d)
    qi = pl.program_id(2)
    q0 = qi * tq
    head = lambda r: slice(r * tq, (r + 1) * tq)

    def stage_kv(g):
        sect = lambda n: slice((n * g_kv + g) * d, (n * g_kv + g + 1) * d)
        ksx_ref[:, 0:d] = _rms(kv_ref[:, sect(0)].astype(F32), kg_ref[1:2, :]).astype(BF16)
        ksx_ref[:, d:2 * d] = (_iota((seq, d), 0) >> SLC_SHIFT == _iota((seq, d), 1)).astype(BF16)
        kwn_ref[...] = _rms(kv_ref[:, sect(2)].astype(F32), kg_ref[2:3, :]).astype(BF16)
        for c in range(seq // tk):
            vst_ref[c] = kv_ref[c * tk:(c + 1) * tk, sect(1)].astype(F32).T.astype(BF16)
        for c in range(seq // tq):
            vwt_ref[c] = kv_ref[c * tq:(c + 1) * tq, sect(3)].astype(F32).T.astype(BF16)

    for g in range(g_kv):
        pl.when((qi == 0) & (pl.program_id(1) == g))(functools.partial(stage_kv, g))

    qraw = q_ref[...].astype(F32)
    qt = jnp.concatenate(
        [(_rms(qraw[:, r * d:(r + 1) * d], qg_ref[...]) * (d ** -0.5 * LOG2_E)).T for r in range(r_)],
        axis=1).astype(BF16)

    sc = _dot(kc_ref[...], qt)
    mask_c = _iota((nc, tq), 0) * CMP_STRIDE + (CMP_BLOCK - 1) <= q0 + _iota((nc, tq), 1)
    o_cmp = []
    psum = None
    for r in range(r_):
        p, den = _col_softmax(sc[:, head(r)], mask_c, may_be_empty=True)
        p = p / jnp.maximum(den, 1e-30)
        psum = p if psum is None else psum + p
        o_cmp.append(_dot(vct_ref[...], p.astype(BF16)))

    overlap = ov_ref[...]
    hi = psum.astype(BF16)
    r1 = psum - hi.astype(F32)
    lo = r1.astype(BF16)
    lo2 = (r1 - lo.astype(F32)).astype(BF16)
    imp = _dot(overlap, hi) + _dot(overlap, lo) + _dot(overlap, lo2)

    j_s = _iota((n_slc, tq), 0)
    pos_s = q0 + _iota((n_slc, tq), 1)
    dlt = (pos_s >> SLC_SHIFT) - j_s
    forced = (j_s == 0) | ((dlt >= 0) & (dlt < N_LOCAL_SLC))
    val = jnp.where(dlt >= 0, imp, NEG)
    val = jnp.where(forced, FORCE_SCORE, val)
    sub = F32_SUBLANES
    groups = [val[g0:g0 + sub, :] for g0 in range(0, n_slc, sub)]
    cnts = [jnp.zeros((sub, tq), jnp.int32) for _ in groups]
    j_g = _iota((sub, tq), 0)
    for i in range(n_slc):
        vi = val[i:i + 1, :]
        for gi, vg in enumerate(groups):
            g0 = gi * sub
            if g0 > i:
                beats = vi >= vg
            elif g0 + sub - 1 < i:
                beats = vi > vg
            else:
                beats = (vi > vg) | ((vi == vg) & (j_g + g0 > i))
            cnts[gi] = cnts[gi] + beats.astype(jnp.int32)
    cnt = jnp.concatenate(cnts, axis=0)
    bias = jnp.where(cnt < n_top, 0.0, NEG)
    if n_slc < d:
        bias = jnp.concatenate([bias, jnp.zeros((d - n_slc, tq), F32)], axis=0)
    qx = jnp.concatenate([qt, jnp.concatenate([bias.astype(BF16)] * r_, axis=1)], axis=0)

    m_ref[...] = jnp.full((1, cols), NEG, F32)
    l_ref[...] = jnp.zeros((1, cols), F32)
    acc_ref[...] = jnp.zeros((d, cols), F32)

    def scores(kt):
        k0 = pl.multiple_of(kt * tk, tk)
        return _dot(ksx_ref[pl.ds(k0, tk), :], qx)

    def fold(kt, r, s, causal):
        if causal is not None:
            s = jnp.where(causal >= kt * tk, s, NEG)
        m_old = m_ref[:, head(r)]
        m_new = jnp.maximum(m_old, jnp.max(s, axis=0, keepdims=True))
        alpha = jnp.exp2(m_old - m_new)
        p = jnp.exp2(s - m_new)
        l_ref[:, head(r)] = alpha * l_ref[:, head(r)] + jnp.sum(p, axis=0, keepdims=True)
        acc_ref[:, head(r)] = alpha * acc_ref[:, head(r)] + _dot(vst_ref[kt], p.astype(BF16))
        m_ref[:, head(r)] = m_new

    def slc_tiles(kts, causal):
        s_all = [scores(kt) for kt in kts]
        for kt, s in zip(kts, s_all):
            for r in range(r_):
                fold(kt, r, s[:, head(r)], causal)

    n_quads = q0 // (4 * tk)

    def quad(i, carry):
        slc_tiles([4 * i + dd for dd in range(4)], None)
        return carry

    lax.fori_loop(0, n_quads, quad, 0)
    causal_gap = q0 + _iota((tk, tq), 1) - _iota((tk, tq), 0)
    n_tiles = (q0 + tq + tk - 1) // tk

    def causal_pair(i, carry):
        slc_tiles([4 * n_quads + 2 * i, 4 * n_quads + 2 * i + 1], causal_gap)
        return carry

    lax.fori_loop(0, (n_tiles - 4 * n_quads + 1) // 2, causal_pair, 0)

    nw = WINDOW // tq + 1
    wc = jnp.maximum(qi - WINDOW // tq, 0)
    w0 = pl.multiple_of(wc * tq, tq)
    sw = _dot(kwn_ref[pl.ds(w0, nw * tq), :], qt)
    gap_w = q0 + _iota((nw * tq, tq), 1) - (w0 + _iota((nw * tq, tq), 0))
    mask_w = (gap_w >= 0) & (gap_w < WINDOW)
    o_win = []
    for r in range(r_):
        p, den = _col_softmax(sw[:, head(r)], mask_w, may_be_empty=False)
        p = p.astype(BF16)
        o = _dot(vwt_ref[wc], p[0:tq, :])
        for c in range(1, nw):
            o = o + _dot(vwt_ref[wc + c], p[c * tq:(c + 1) * tq, :])
        o_win.append(o / den)

    gate = jax.nn.sigmoid(gl_ref[...].astype(F32)).T
    for r in range(r_):
        c = N_BRANCH * r
        o_slc = acc_ref[:, head(r)] / l_ref[:, head(r)]
        o = (gate[c:c + 1, :] * o_cmp[r] + gate[c + 1:c + 2, :] * o_slc
             + gate[c + 2:c + 3, :] * o_win[r])
        o_ref[:, r * d:(r + 1) * d] = o.T.astype(o_ref.dtype)


def nsa_attention(p, kc, vct, q_gain, k_gain, *, batch, seq, col, tq, tk):
    b, g, nc, d = kc.shape
    r_ = NSA_GROUP
    nq = seq // tq
    n_slc = seq // SLC_BLOCK
    assert seq % tq == 0 and seq % tk == 0 and WINDOW % tq == 0 and WINDOW + tq <= seq
    assert tq == tk and nq % 2 == 0 and n_slc <= d
    cols = r_ * tq
    cmp_start = jnp.arange(nc)[None, :] * CMP_STRIDE
    slc_start = jnp.arange(n_slc)[:, None] * SLC_BLOCK
    overlap = ((cmp_start < slc_start + SLC_BLOCK) & (cmp_start + CMP_BLOCK > slc_start)).astype(BF16)
    kv4 = 4 * g * d
    return pl.pallas_call(
        functools.partial(_nsa_body, tq=tq, tk=tk, n_slc=n_slc, n_top=min(N_SLC, n_slc)),
        grid=(batch, g, nq),
        in_specs=[pl.BlockSpec((tq, r_ * d), lambda bi, gi, qi: (bi * nq + qi, col["q"] // r_ + gi)),
                  pl.BlockSpec((None, None, nc, d), lambda bi, gi, qi: (bi, gi, 0, 0)),
                  pl.BlockSpec((None, None, d, nc), lambda bi, gi, qi: (bi, gi, 0, 0)),
                  _resident((n_slc, nc), lambda bi, gi, qi: (0, 0)),
                  pl.BlockSpec((seq, kv4), lambda bi, gi, qi: (bi, col["kv4"])),
                  pl.BlockSpec((tq, LANES), lambda bi, gi, qi: (bi * nq + qi, col["gate"] + gi)),
                  pl.BlockSpec((1, d), lambda bi, gi, qi: (0, 0)),
                  pl.BlockSpec((N_BRANCH, d), lambda bi, gi, qi: (0, 0))],
        out_specs=pl.BlockSpec((tq, r_ * d), lambda bi, gi, qi: (bi * nq + qi, gi)),
        out_shape=jax.ShapeDtypeStruct((batch * seq, g * r_ * d), BF16),
        scratch_shapes=[pltpu.VMEM((seq, 2 * d), BF16), pltpu.VMEM((seq, d), BF16),
                        pltpu.VMEM((seq // tk, d, tk), BF16), pltpu.VMEM((seq // tq, d, tq), BF16),
                        pltpu.VMEM((1, cols), F32), pltpu.VMEM((1, cols), F32),
                        pltpu.VMEM((d, cols), F32)],
        compiler_params=_params("parallel", "arbitrary", "arbitrary"),
        name="nsa_attention",
    )(p, kc, vct, overlap, p, p, q_gain.reshape(1, d), k_gain)


def _pad_cols(w, n):
    return jnp.pad(w, ((0, 0), (0, n - w.shape[1])))


def _even_mixer(x, batch, seq, norm_g, w_in, q_gain, k_gain, cmp_pe, kw1, kw2, vw1, vw2,
                gmlp_norm, gmlp_ws, gmlp_b, casts):
    d = HEAD_DIM
    dm = x.shape[1]
    gmlp_w = gmlp_ws.shape[0] * GMLP_GROUP_DIM
    qw = dm - gmlp_w
    g_kv = max(1, qw // d // NSA_GROUP)
    kv = g_kv * d
    n_kv = 2 * N_BRANCH
    n_gate = N_BRANCH * qw // d
    assert w_in.shape[1] == qw + n_kv * kv + n_gate + 2 * gmlp_w
    o_gl = qw + n_kv * kv
    o_u = o_gl + n_gate
    per_g = n_gate // g_kv
    gate_cols = [_pad_cols(w_in[:, o_gl + gi * per_g: o_gl + (gi + 1) * per_g], LANES) for gi in range(g_kv)]
    w_nsa = jnp.concatenate([w_in[:, :qw], w_in[:, qw + 2 * kv:o_gl], w_in[:, qw:qw + 2 * kv]] + gate_cols, axis=1)
    u_blk = -(-w_nsa.shape[1] // gmlp_w)
    w_cat = jnp.concatenate([_pad_cols(w_nsa, u_blk * gmlp_w), w_in[:, o_u:]], axis=1).astype(BF16)
    tn = 512
    assert w_cat.shape[1] % tn == 0 and qw % (4 * kv) == 0
    col = {"q": 0, "kv4": qw // (4 * kv), "kc": (qw + 4 * kv) // d, "vc": (qw + 5 * kv) // d, "gate": o_gl // d}

    p, cast_w = norm_matmul(x, norm_g, w_cat, tm=1024, tn=tn, rc=256, casts=casts)

    hd = CMP_STRIDE * d
    kc, vct = compress(p, cmp_pe, kw1.reshape(2, hd, d).astype(BF16), kw2.astype(BF16),
                       vw1.reshape(2, hd, d).astype(BF16), vw2.astype(BF16), k_gain[0:1],
                       batch=batch, seq=seq, g_kv=g_kv, k_blk=col["kc"], v_blk=col["vc"])
    oa = nsa_attention(p, kc, vct, q_gain, k_gain, batch=batch, seq=seq, col=col, tq=256, tk=256)
    ob = gmlp(p, gmlp_norm, gmlp_ws, gmlp_b, u_blk=u_blk, v_blk=u_blk + 1, tt=512)
    return oa, ob, cast_w


def kernel(x, ev_norm, ev_w_in, ev_q_gain, ev_k_gain, ev_cmp_pe, ev_cmp_k_w1, ev_cmp_k_w2, ev_cmp_v_w1, ev_cmp_v_w2, ev_gmlp_norm, ev_gmlp_ws, ev_gmlp_b, ev_w_out, od_norm, od_w_in, od_conv_w, od_w_out, ffn_norm, ffn_w_in, ffn_conv_w, ffn_conv_b, ffn_w_down):
    batch, seq, dm = x.shape
    depth = ffn_norm.shape[0]
    no_bias = jnp.zeros((od_conv_w.shape[0], od_conv_w.shape[2]), F32)
    in_tiles = dict(seq=seq, tm=1024, tn=512, rc=256)
    h = x.reshape(batch * seq, dm)
    ffn_in_w = od_in_w = None
    for i in range(depth):
        j = i // 2
        if i % 2 == 0:
            casts = [(ev_w_out, j)] + ([(ffn_w_in, i)] if ffn_in_w is None else [])
            oa, ob, cast_w = _even_mixer(h, batch, seq, ev_norm[j], ev_w_in[j], ev_q_gain[j], ev_k_gain[j],
                                         ev_cmp_pe[j], ev_cmp_k_w1[j], ev_cmp_k_w2[j], ev_cmp_v_w1[j],
                                         ev_cmp_v_w2[j], ev_gmlp_norm[j], ev_gmlp_ws[j], ev_gmlp_b[j], casts)
            ffn_in_w = cast_w[1] if ffn_in_w is None else ffn_in_w
            h, _ = matmul_residual(h, [oa, ob], cast_w[0], tm=512)
        else:
            m, (out_w,) = gated_in_proj(h, od_norm, od_in_w, 0, od_conv_w, no_bias, layer=j, mode="sconv",
                                        casts=[(od_w_out, j)], **in_tiles)
            h, _ = matmul_residual(h, [m], out_w, tm=512)
        next_odd = i + 1 < depth and (i + 1) % 2 == 1
        casts = [(ffn_w_down, i)] + ([(od_w_in, (i + 1) // 2)] if next_odd else [])
        f, cast_w = gated_in_proj(h, ffn_norm, ffn_in_w, 0, ffn_conv_w, ffn_conv_b, layer=i, mode="ffn",
                                  casts=casts, **in_tiles)
        od_in_w = cast_w[1] if next_odd else None
        h, cast_next = matmul_residual(h, [f], cast_w[0], tm=256,
                                       casts=[(ffn_w_in, i + 1)] if i + 1 < depth else [])
        ffn_in_w = cast_next[0] if cast_next else None
    return h.reshape(batch, seq, dm)
```

```python
import functools

import jax
import jax.numpy as jnp
from jax import lax
from jax.experimental import pallas as pl
from jax.experimental.pallas import tpu as pltpu

F32 = jnp.float32
BF16 = jnp.bfloat16

HEAD_DIM = 128
NSA_GROUP = 4
N_BRANCH = 3
CMP_BLOCK = 32
CMP_STRIDE = 16
SLC_BLOCK = 64
SLC_SHIFT = SLC_BLOCK.bit_length() - 1
LOG2_E = 1.4426950408889634
QK_AHEAD = 2
N_SLC = 16
N_LOCAL_SLC = 2
WINDOW = 512
FORCE_SCORE = 1e9
GMLP_GROUP_DIM = 128
GMLP_CHUNK = 128
CONV_WIDTH = 3
EPS = 1e-6
NEG = -1e30

LANES = 128
F32_SUBLANES = 8
BF16_SUBLANES = 16
VMEM_LIMIT = 56 * 1024 * 1024


def _params(*sem):
    return pltpu.CompilerParams(dimension_semantics=sem, vmem_limit_bytes=VMEM_LIMIT)


def _dot(a, b):
    return jnp.dot(a, b, preferred_element_type=F32)


def _rms(x, gain):
    return x * lax.rsqrt(jnp.mean(x * x, axis=-1, keepdims=True) + EPS) * gain


def _iota(shape, axis):
    return lax.broadcasted_iota(jnp.int32, shape, axis)


def _resident(shape, index_map):
    return pl.BlockSpec(shape, index_map, pipeline_mode=pl.Buffered(1))


def _cast_row_blocks(rows, n_steps):
    for nb in range(min(n_steps, rows // BF16_SUBLANES), 0, -1):
        if rows % nb == 0 and (rows // nb) % BF16_SUBLANES == 0:
            return nb
    raise ValueError(f"cannot split {rows} rows")


def _call_with_casts(body, *, grid, in_specs, out_spec, out_shape, scratch_shapes, sem, name, args, casts):
    n_steps = 1
    for n in grid:
        n_steps *= n

    def step(*idx):
        s = idx[0]
        for k, n in zip(idx[1:], grid[1:]):
            s = s * n + k
        return s

    n_in, n_cast = len(in_specs), len(casts)
    src_specs, dst_specs, dst_shapes = [], [], []
    for w, layer in casts:
        _, rows, c = w.shape
        nb = _cast_row_blocks(rows, n_steps)
        blk = functools.partial(lambda *idx, nb: jnp.minimum(step(*idx), nb - 1), nb=nb)
        src_specs.append(pl.BlockSpec((None, rows // nb, c),
                                      functools.partial(lambda *idx, blk, layer: (layer, blk(*idx), 0),
                                                        blk=blk, layer=layer)))
        dst_specs.append(pl.BlockSpec((None, rows // nb, c),
                                      functools.partial(lambda *idx, blk: (0, blk(*idx), 0), blk=blk)))
        dst_shapes.append(jax.ShapeDtypeStruct((1, rows, c), BF16))

    def with_casts(*refs):
        ins, src, rest = refs[:n_in], refs[n_in:n_in + n_cast], refs[n_in + n_cast:]
        out, dst, scratch = rest[0], rest[1:1 + n_cast], rest[1 + n_cast:]
        for s_ref, d_ref in zip(src, dst):
            d_ref[...] = s_ref[...].astype(BF16)
        body(*ins, out, *scratch)

    outs = pl.pallas_call(
        with_casts,
        grid=grid,
        in_specs=list(in_specs) + src_specs,
        out_specs=[out_spec] + dst_specs,
        out_shape=[out_shape] + dst_shapes,
        scratch_shapes=scratch_shapes,
        compiler_params=_params(*sem),
        name=name,
    )(*args, *[w for w, _ in casts])
    return outs[0], list(outs[1:])


def _normed_rows(x_ref, g_ref, xn_ref, rows, first):
    if not first:
        return xn_ref[rows, :]
    xc = _rms(x_ref[rows, :], g_ref[...]).astype(BF16)
    xn_ref[rows, :] = xc
    return xc


def _first_then_rest(j, body):
    pl.when(j == 0)(functools.partial(body, True))
    pl.when(j != 0)(functools.partial(body, False))


def _norm_matmul_body(x_ref, g_ref, w_ref, o_ref, xn_ref, *, rc):
    def body(first):
        for c in range(x_ref.shape[0] // rc):
            rows = slice(c * rc, (c + 1) * rc)
            xc = _normed_rows(x_ref, g_ref, xn_ref, rows, first)
            o_ref[rows, :] = _dot(xc, w_ref[...]).astype(o_ref.dtype)

    _first_then_rest(pl.program_id(1), body)


def norm_matmul(x, g, w, *, tm, tn, rc, casts=()):
    t, d = x.shape
    n = w.shape[1]
    assert t % tm == 0 and n % tn == 0 and tm % rc == 0
    return _call_with_casts(
        functools.partial(_norm_matmul_body, rc=rc),
        grid=(t // tm, n // tn),
        in_specs=[pl.BlockSpec((tm, d), lambda i, j: (i, 0)),
                  _resident((1, d), lambda i, j: (0, 0)),
                  (_resident if n == tn else pl.BlockSpec)((d, tn), lambda i, j: (0, j))],
        out_spec=pl.BlockSpec((tm, tn), lambda i, j: (i, j)),
        out_shape=jax.ShapeDtypeStruct((t, n), BF16),
        scratch_shapes=[pltpu.VMEM((tm, d), BF16)],
        sem=("arbitrary", "arbitrary"),
        name="norm_matmul",
        args=(x, g.reshape(1, d), w),
        casts=casts)


def _shift_rows(cur, tail, k):
    out = pltpu.roll(cur, k, axis=0)
    row = _iota(cur.shape, 0)
    for r in range(k):
        src = F32_SUBLANES - k + r
        out = jnp.where(row == r, tail[src:src + 1, :], out)
    return out


def _conv3(cur, tail, cw):
    return (cw[0:1, :] * _shift_rows(cur, tail, 2) + cw[1:2, :] * _shift_rows(cur, tail, 1)
            + cw[2:3, :] * cur)


def _gated_in_body(x_ref, gn_ref, *rest, mode, rc, seq_tiles):
    n_parts = 2 if mode == "ffn" else 3
    w_refs = rest[:n_parts]
    cw_ref, cb_ref, h_ref, xn_ref, tail_ref = rest[n_parts:]
    i, j = pl.program_id(0), pl.program_id(1)
    tm = x_ref.shape[0]

    @pl.when(i % seq_tiles == 0)
    def _():
        tail_ref[j] = jnp.zeros(tail_ref.shape[1:], F32)

    def body(first):
        cw = cw_ref[...]
        tail = tail_ref[j]
        for c in range(tm // rc):
            rows = slice(c * rc, (c + 1) * rc)
            xc = _normed_rows(x_ref, gn_ref, xn_ref, rows, first)
            parts = [_dot(xc, w_ref[...]) for w_ref in w_refs]
            if mode == "ffn":
                g, u = parts
                h = jax.nn.silu(_conv3(g, tail, cw) + cb_ref[...]) * u
                tail = g[rc - F32_SUBLANES:, :]
            else:
                a, cc, dd = parts
                m = cc * dd
                h = a * _conv3(m, tail, cw)
                tail = m[rc - F32_SUBLANES:, :]
            h_ref[rows, :] = h.astype(h_ref.dtype)
        tail_ref[j] = tail

    _first_then_rest(j, body)


def gated_in_proj(x, g, w, w_layer, conv_w, conv_b, *, layer, mode, seq, tm, tn, rc, casts=()):
    t, d = x.shape
    n_parts = 2 if mode == "ffn" else 3
    f = w.shape[2] // n_parts
    assert t % tm == 0 and f % tn == 0 and seq % tm == 0 and tm % rc == 0
    nj = f // tn
    w_specs = [pl.BlockSpec((None, d, tn), functools.partial(lambda i, j, p: (w_layer, 0, p * nj + j), p=p))
               for p in range(n_parts)]
    return _call_with_casts(
        functools.partial(_gated_in_body, mode=mode, rc=rc, seq_tiles=seq // tm),
        grid=(t // tm, nj),
        in_specs=[pl.BlockSpec((tm, d), lambda i, j: (i, 0)),
                  _resident((None, 1, d), lambda i, j: (layer, 0, 0))] + w_specs + [
                  pl.BlockSpec((None, CONV_WIDTH, tn), lambda i, j: (layer, 0, j)),
                  pl.BlockSpec((None, 1, tn), lambda i, j: (layer, 0, j))],
        out_spec=pl.BlockSpec((tm, tn), lambda i, j: (i, j)),
        out_shape=jax.ShapeDtypeStruct((t, f), BF16),
        scratch_shapes=[pltpu.VMEM((tm, d), BF16), pltpu.VMEM((nj, F32_SUBLANES, tn), F32)],
        sem=("arbitrary", "arbitrary"),
        name=mode + "_in_proj",
        args=(x, g[:, None, :], *([w] * n_parts), conv_w, conv_b[:, None, :]),
        casts=casts)


def _matmul_residual_body(x_ref, *rest):
    n = (len(rest) - 1) // 2
    h_refs, w_refs, o_ref = rest[:n], rest[n:2 * n], rest[2 * n]
    acc = x_ref[...]
    for h_ref, w_ref in zip(h_refs, w_refs):
        acc = acc + _dot(h_ref[...], w_ref[...])
    o_ref[...] = acc


def matmul_residual(x, hs, w, *, tm, casts=()):
    t, d = x.shape
    k = hs[0].shape[1]
    assert all(h.shape == (t, k) for h in hs) and w.shape == (1, k * len(hs), d) and t % tm == 0
    h_specs = [pl.BlockSpec((tm, k), lambda i: (i, 0)) for _ in hs]
    w_specs = [_resident((None, k, d), functools.partial(lambda i, p: (0, p, 0), p=p))
               for p in range(len(hs))]
    return _call_with_casts(
        _matmul_residual_body,
        grid=(t // tm,),
        in_specs=[pl.BlockSpec((tm, d), lambda i: (i, 0))] + h_specs + w_specs,
        out_spec=pl.BlockSpec((tm, d), lambda i: (i, 0)),
        out_shape=jax.ShapeDtypeStruct((t, d), F32),
        scratch_shapes=[],
        sem=("arbitrary",),
        name="out_proj",
        args=(x, *hs, *([w] * len(hs))),
        casts=casts)


def _gmlp_body(u_ref, v_ref, gain_ref, ws_ref, bs_ref, o_ref, *, chunks):
    c, gd = GMLP_CHUNK, GMLP_GROUP_DIM
    causal = _iota((c, c), 1) <= _iota((c, c), 0)
    for g in range(ws_ref.shape[0]):
        cols = slice(g * gd, (g + 1) * gd)
        v = jax.nn.gelu(v_ref[:, cols].astype(F32))
        vc = v - jnp.mean(v, axis=-1, keepdims=True)
        vn = vc * lax.rsqrt(jnp.mean(vc * vc, axis=-1, keepdims=True) + EPS) * gain_ref[g]
        vn = vn.astype(BF16)
        w = jnp.where(causal, ws_ref[g], 0.0).astype(BF16)
        bias = bs_ref[g]
        for ci in range(chunks):
            rows = slice(ci * c, (ci + 1) * c)
            mixed = _dot(w, vn[rows, :]) + bias
            u = jax.nn.gelu(u_ref[rows, cols].astype(F32))
            o_ref[rows, cols] = (u * mixed).astype(o_ref.dtype)


def gmlp(p, gain, ws, bs, *, u_blk, v_blk, tt):
    t = p.shape[0]
    groups, c, _ = ws.shape
    gd = GMLP_GROUP_DIM
    width = groups * gd
    assert t % tt == 0 and tt % c == 0
    full = lambda shape: _resident(shape, lambda i: (0,) * len(shape))
    return pl.pallas_call(
        functools.partial(_gmlp_body, chunks=tt // c),
        grid=(t // tt,),
        in_specs=[pl.BlockSpec((tt, width), lambda i: (i, u_blk)),
                  pl.BlockSpec((tt, width), lambda i: (i, v_blk)),
                  full((groups, 1, gd)), full((groups, c, c)), full((groups, c, 1))],
        out_specs=pl.BlockSpec((tt, width), lambda i: (i, 0)),
        out_shape=jax.ShapeDtypeStruct((t, width), BF16),
        compiler_params=_params("parallel"),
        name="gmlp",
    )(p, p, gain.reshape(groups, 1, gd), ws, bs.reshape(groups, c, 1))


def _compress_body(k_ref, v_ref, pe_ref, kw1_ref, kw2_ref, vw1_ref, vw2_ref, kg_ref, kc_ref, vct_ref, x_ref):
    st = CMP_STRIDE
    n = x_ref.shape[0] // st

    def mlp(src_ref, w1_ref, w2_ref):
        x_ref[...] = src_ref[...].astype(F32)

        def half(part):
            return jnp.concatenate(
                [(x_ref[pl.ds(l, n, stride=st), :] + pe_ref[part * st + l:part * st + l + 1, :]).astype(BF16)
                 for l in range(st)], axis=1)

        a = _dot(half(0), w1_ref[0])
        b = _dot(half(1), w1_ref[1])
        hid = jax.nn.gelu(a + pltpu.roll(b, n - 1, axis=0))
        return _dot(hid.astype(BF16), w2_ref[...])

    kc_ref[...] = _rms(mlp(k_ref, kw1_ref, kw2_ref), kg_ref[...]).astype(kc_ref.dtype)
    vct_ref[...] = mlp(v_ref, vw1_ref, vw2_ref).T.astype(vct_ref.dtype)


def compress(p, pe, kw1, kw2, vw1, vw2, kgain, *, batch, seq, g_kv, k_blk, v_blk):
    d = HEAD_DIM
    st = CMP_STRIDE
    n = seq // st
    hd = st * d
    assert CMP_BLOCK == 2 * st and pe.shape == (CMP_BLOCK, d) and kw1.shape == (2, hd, d)
    full = lambda shape: pl.BlockSpec(shape, lambda i, j: (0,) * len(shape))
    return pl.pallas_call(
        _compress_body,
        grid=(batch, g_kv),
        in_specs=[pl.BlockSpec((seq, d), lambda i, j: (i, k_blk + j)),
                  pl.BlockSpec((seq, d), lambda i, j: (i, v_blk + j)),
                  full((CMP_BLOCK, d)), full((2, hd, d)), full((d, d)),
                  full((2, hd, d)), full((d, d)), full((1, d))],
        out_specs=[pl.BlockSpec((None, None, n, d), lambda i, j: (i, j, 0, 0)),
                   pl.BlockSpec((None, None, d, n), lambda i, j: (i, j, 0, 0))],
        out_shape=[jax.ShapeDtypeStruct((batch, g_kv, n, d), BF16),
                   jax.ShapeDtypeStruct((batch, g_kv, d, n), BF16)],
        scratch_shapes=[pltpu.VMEM((seq, d), F32)],
        compiler_params=_params("parallel", "arbitrary"),
        name="nsa_compress",
    )(p, p, pe, kw1, kw2, vw1, vw2, kgain)


def _col_softmax(s, mask, *, may_be_empty):
    s = jnp.where(mask, s, NEG)
    p = jnp.exp2(s - jnp.max(s, axis=0, keepdims=True))
    if may_be_empty:
        p = jnp.where(mask, p, 0.0)
    return p, jnp.sum(p, axis=0, keepdims=True)


def _nsa_body(q_ref, kc_ref, vct_ref, ov_ref, kv_ref, gl_ref, qg_ref, kg_ref, o_ref,
              ksx_ref, kwn_ref, vst_ref, vwt_ref, m_ref, l_ref, acc_ref, *, tq, tk, n_slc, n_top):
    r_, d = NSA_GROUP, HEAD_DIM
    cols = r_ * tq
    nc = kc_ref.shape[0]
    seq = kv_ref.shape[0]
    g_kv = kv_ref.shape[1] // (4 * d)
    qi = pl.program_id(2)
    q0 = qi * tq
    head = lambda r: slice(r * tq, (r + 1) * tq)

    def stage_kv(g):
        sect = lambda n: slice((n * g_kv + g) * d, (n * g_kv + g + 1) * d)
        ksx_ref[:, 0:d] = _rms(kv_ref[:, sect(0)].astype(F32), kg_ref[1:2, :]).astype(BF16)
        ksx_ref[:, d:2 * d] = (_iota((seq, d), 0) >> SLC_SHIFT == _iota((seq, d), 1)).astype(BF16)
        kwn_ref[...] = _rms(kv_ref[:, sect(2)].astype(F32), kg_ref[2:3, :]).astype(BF16)
        for c in range(seq // tk):
            vst_ref[c] = kv_ref[c * tk:(c + 1) * tk, sect(1)].astype(F32).T.astype(BF16)
        for c in range(seq // tq):
            vwt_ref[c] = kv_ref[c * tq:(c + 1) * tq, sect(3)].astype(F32).T.astype(BF16)

    for g in range(g_kv):
        pl.when((qi == 0) & (pl.program_id(1) == g))(functools.partial(stage_kv, g))

    qraw = q_ref[...].astype(F32)
    qt = jnp.concatenate(
        [(_rms(qraw[:, r * d:(r + 1) * d], qg_ref[...]) * (d ** -0.5 * LOG2_E)).T for r in range(r_)],
        axis=1).astype(BF16)

    sc = _dot(kc_ref[...], qt)
    mask_c = _iota((nc, tq), 0) * CMP_STRIDE + (CMP_BLOCK - 1) <= q0 + _iota((nc, tq), 1)
    o_cmp = []
    psum = None
    for r in range(r_):
        p, den = _col_softmax(sc[:, head(r)], mask_c, may_be_empty=True)
        p = p / jnp.maximum(den, 1e-30)
        psum = p if psum is None else psum + p
        o_cmp.append(_dot(vct_ref[...], p.astype(BF16)))

    overlap = ov_ref[...]
    hi = psum.astype(BF16)
    r1 = psum - hi.astype(F32)
    lo = r1.astype(BF16)
    lo2 = (r1 - lo.astype(F32)).astype(BF16)
    imp = _dot(overlap, hi) + _dot(overlap, lo) + _dot(overlap, lo2)

    j_s = _iota((n_slc, tq), 0)
    pos_s = q0 + _iota((n_slc, tq), 1)
    dlt = (pos_s >> SLC_SHIFT) - j_s
    forced = (j_s == 0) | ((dlt >= 0) & (dlt < N_LOCAL_SLC))
    val = jnp.where(dlt >= 0, imp, NEG)
    val = jnp.where(forced, FORCE_SCORE, val)
    sub = F32_SUBLANES
    groups = [val[g0:g0 + sub, :] for g0 in range(0, n_slc, sub)]
    cnts = [jnp.zeros((sub, tq), jnp.int32) for _ in groups]
    j_g = _iota((sub, tq), 0)
    for i in range(n_slc):
        vi = val[i:i + 1, :]
        for gi, vg in enumerate(groups):
            g0 = gi * sub
            if g0 > i:
                beats = vi >= vg
            elif g0 + sub - 1 < i:
                beats = vi > vg
            else:
                beats = (vi > vg) | ((vi == vg) & (j_g + g0 > i))
            cnts[gi] = cnts[gi] + beats.astype(jnp.int32)
    cnt = jnp.concatenate(cnts, axis=0)
    bias = jnp.where(cnt < n_top, 0.0, NEG)
    if n_slc < d:
        bias = jnp.concatenate([bias, jnp.zeros((d - n_slc, tq), F32)], axis=0)
    qx = jnp.concatenate([qt, jnp.concatenate([bias.astype(BF16)] * r_, axis=1)], axis=0)

    m_ref[...] = jnp.full((1, cols), NEG, F32)
    l_ref[...] = jnp.zeros((1, cols), F32)
    acc_ref[...] = jnp.zeros((d, cols), F32)

    def scores(kt):
        k0 = pl.multiple_of(kt * tk, tk)
        return _dot(ksx_ref[pl.ds(k0, tk), :], qx)

    def fold(kt, r, s, causal):
        if causal is not None:
            s = jnp.where(causal >= kt * tk, s, NEG)
        m_old = m_ref[:, head(r)]
        m_new = jnp.maximum(m_old, jnp.max(s, axis=0, keepdims=True))
        alpha = jnp.exp2(m_old - m_new)
        p = jnp.exp2(s - m_new)
        l_ref[:, head(r)] = alpha * l_ref[:, head(r)] + jnp.sum(p, axis=0, keepdims=True)
        acc_ref[:, head(r)] = alpha * acc_ref[:, head(r)] + _dot(vst_ref[kt], p.astype(BF16))
        m_ref[:, head(r)] = m_new

    def slc_tiles(kts, causal):
        s_all = [scores(kt) for kt in kts]
        for kt, s in zip(kts, s_all):
            for r in range(r_):
                fold(kt, r, s[:, head(r)], causal)

    n_quads = q0 // (4 * tk)

    def quad(i, carry):
        slc_tiles([4 * i + dd for dd in range(4)], None)
        return carry

    lax.fori_loop(0, n_quads, quad, 0)
    causal_gap = q0 + _iota((tk, tq), 1) - _iota((tk, tq), 0)
    n_tiles = (q0 + tq + tk - 1) // tk

    def causal_pair(i, carry):
        slc_tiles([4 * n_quads + 2 * i, 4 * n_quads + 2 * i + 1], causal_gap)
        return carry

    lax.fori_loop(0, (n_tiles - 4 * n_quads + 1) // 2, causal_pair, 0)

    nw = WINDOW // tq + 1
    wc = jnp.maximum(qi - WINDOW // tq, 0)
    w0 = pl.multiple_of(wc * tq, tq)
    sw = _dot(kwn_ref[pl.ds(w0, nw * tq), :], qt)
    gap_w = q0 + _iota((nw * tq, tq), 1) - (w0 + _iota((nw * tq, tq), 0))
    mask_w = (gap_w >= 0) & (gap_w < WINDOW)
    o_win = []
    for r in range(r_):
        p, den = _col_softmax(sw[:, head(r)], mask_w, may_be_empty=False)
        p = p.astype(BF16)
        o = _dot(vwt_ref[wc], p[0:tq, :])
        for c in range(1, nw):
            o = o + _dot(vwt_ref[wc + c], p[c * tq:(c + 1) * tq, :])
        o_win.append(o / den)

    gate = jax.nn.sigmoid(gl_ref[...].astype(F32)).T
    for r in range(r_):
        c = N_BRANCH * r
        o_slc = acc_ref[:, head(r)] / l_ref[:, head(r)]
        o = (gate[c:c + 1, :] * o_cmp[r] + gate[c + 1:c + 2, :] * o_slc
             + gate[c + 2:c + 3, :] * o_win[r])
        o_ref[:, r * d:(r + 1) * d] = o.T.astype(o_ref.dtype)


def nsa_attention(p, kc, vct, q_gain, k_gain, *, batch, seq, col, tq, tk):
    b, g, nc, d = kc.shape
    r_ = NSA_GROUP
    nq = seq // tq
    n_slc = seq // SLC_BLOCK
    assert seq % tq == 0 and seq % tk == 0 and WINDOW % tq == 0 and WINDOW + tq <= seq
    assert ((tq == tk and nq % 2 == 0) or tq == 2 * tk) and n_slc <= d
    cols = r_ * tq
    cmp_start = jnp.arange(nc)[None, :] * CMP_STRIDE
    slc_start = jnp.arange(n_slc)[:, None] * SLC_BLOCK
    overlap = ((cmp_start < slc_start + SLC_BLOCK) & (cmp_start + CMP_BLOCK > slc_start)).astype(BF16)
    kv4 = 4 * g * d
    return pl.pallas_call(
        functools.partial(_nsa_body, tq=tq, tk=tk, n_slc=n_slc, n_top=min(N_SLC, n_slc)),
        grid=(batch, g, nq),
        in_specs=[pl.BlockSpec((tq, r_ * d), lambda bi, gi, qi: (bi * nq + qi, col["q"] // r_ + gi)),
                  pl.BlockSpec((None, None, nc, d), lambda bi, gi, qi: (bi, gi, 0, 0)),
                  pl.BlockSpec((None, None, d, nc), lambda bi, gi, qi: (bi, gi, 0, 0)),
                  _resident((n_slc, nc), lambda bi, gi, qi: (0, 0)),
                  pl.BlockSpec((seq, kv4), lambda bi, gi, qi: (bi, col["kv4"])),
                  pl.BlockSpec((tq, LANES), lambda bi, gi, qi: (bi * nq + qi, col["gate"] + gi)),
                  pl.BlockSpec((1, d), lambda bi, gi, qi: (0, 0)),
                  pl.BlockSpec((N_BRANCH, d), lambda bi, gi, qi: (0, 0))],
        out_specs=pl.BlockSpec((tq, r_ * d), lambda bi, gi, qi: (bi * nq + qi, gi)),
        out_shape=jax.ShapeDtypeStruct((batch * seq, g * r_ * d), BF16),
        scratch_shapes=[pltpu.VMEM((seq, 2 * d), BF16), pltpu.VMEM((seq, d), BF16),
                        pltpu.VMEM((seq // tk, d, tk), BF16), pltpu.VMEM((seq // tq, d, tq), BF16),
                        pltpu.VMEM((1, cols), F32), pltpu.VMEM((1, cols), F32),
                        pltpu.VMEM((d, cols), F32)],
        compiler_params=_params("parallel", "arbitrary", "arbitrary"),
        name="nsa_attention",
    )(p, kc, vct, overlap, p, p, q_gain.reshape(1, d), k_gain)


def _pad_cols(w, n):
    return jnp.pad(w, ((0, 0), (0, n - w.shape[1])))


def _even_mixer(x, batch, seq, norm_g, w_in, q_gain, k_gain, cmp_pe, kw1, kw2, vw1, vw2,
                gmlp_norm, gmlp_ws, gmlp_b, casts):
    d = HEAD_DIM
    dm = x.shape[1]
    gmlp_w = gmlp_ws.shape[0] * GMLP_GROUP_DIM
    qw = dm - gmlp_w
    g_kv = max(1, qw // d // NSA_GROUP)
    kv = g_kv * d
    n_kv = 2 * N_BRANCH
    n_gate = N_BRANCH * qw // d
    assert w_in.shape[1] == qw + n_kv * kv + n_gate + 2 * gmlp_w
    o_gl = qw + n_kv * kv
    o_u = o_gl + n_gate
    per_g = n_gate // g_kv
    gate_cols = [_pad_cols(w_in[:, o_gl + gi * per_g: o_gl + (gi + 1) * per_g], LANES) for gi in range(g_kv)]
    w_nsa = jnp.concatenate([w_in[:, :qw], w_in[:, qw + 2 * kv:o_gl], w_in[:, qw:qw + 2 * kv]] + gate_cols, axis=1)
    u_blk = -(-w_nsa.shape[1] // gmlp_w)
    w_cat = jnp.concatenate([_pad_cols(w_nsa, u_blk * gmlp_w), w_in[:, o_u:]], axis=1).astype(BF16)
    tn = w_cat.shape[1]
    assert qw % (4 * kv) == 0
    col = {"q": 0, "kv4": qw // (4 * kv), "kc": (qw + 4 * kv) // d, "vc": (qw + 5 * kv) // d, "gate": o_gl // d}

    p, cast_w = norm_matmul(x, norm_g, w_cat, tm=512, tn=tn, rc=256, casts=casts)

    hd = CMP_STRIDE * d
    kc, vct = compress(p, cmp_pe, kw1.reshape(2, hd, d).astype(BF16), kw2.astype(BF16),
                       vw1.reshape(2, hd, d).astype(BF16), vw2.astype(BF16), k_gain[0:1],
                       batch=batch, seq=seq, g_kv=g_kv, k_blk=col["kc"], v_blk=col["vc"])
    oa = nsa_attention(p, kc, vct, q_gain, k_gain, batch=batch, seq=seq, col=col, tq=512, tk=256)
    ob = gmlp(p, gmlp_norm, gmlp_ws, gmlp_b, u_blk=u_blk, v_blk=u_blk + 1, tt=512)
    return oa, ob, cast_w


def kernel(x, ev_norm, ev_w_in, ev_q_gain, ev_k_gain, ev_cmp_pe, ev_cmp_k_w1, ev_cmp_k_w2, ev_cmp_v_w1, ev_cmp_v_w2, ev_gmlp_norm, ev_gmlp_ws, ev_gmlp_b, ev_w_out, od_norm, od_w_in, od_conv_w, od_w_out, ffn_norm, ffn_w_in, ffn_conv_w, ffn_conv_b, ffn_w_down):
    batch, seq, dm = x.shape
    depth = ffn_norm.shape[0]
    no_bias = jnp.zeros((od_conv_w.shape[0], od_conv_w.shape[2]), F32)
    in_tiles = dict(seq=seq, tm=1024, tn=512, rc=256)
    h = x.reshape(batch * seq, dm)
    ffn_in_w = od_in_w = None
    for i in range(depth):
        j = i // 2
        if i % 2 == 0:
            oa, ob, (out_w,) = _even_mixer(h, batch, seq, ev_norm[j], ev_w_in[j], ev_q_gain[j], ev_k_gain[j],
                                           ev_cmp_pe[j], ev_cmp_k_w1[j], ev_cmp_k_w2[j], ev_cmp_v_w1[j],
                                           ev_cmp_v_w2[j], ev_gmlp_norm[j], ev_gmlp_ws[j], ev_gmlp_b[j],
                                           [(ev_w_out, j)])
            h, cast_w = matmul_residual(h, [oa, ob], out_w, tm=512,
                                        casts=[(ffn_w_in, i)] if ffn_in_w is None else [])
            ffn_in_w = cast_w[0] if cast_w else ffn_in_w
        else:
            m, (out_w,) = gated_in_proj(h, od_norm, od_in_w, 0, od_conv_w, no_bias, layer=j, mode="sconv",
                                        casts=[(od_w_out, j)], **in_tiles)
            h, _ = matmul_residual(h, [m], out_w, tm=512)
        next_odd = i + 1 < depth and (i + 1) % 2 == 1
        casts = [(ffn_w_down, i)] + ([(od_w_in, (i + 1) // 2)] if next_odd else [])
        f, cast_w = gated_in_proj(h, ffn_norm, ffn_in_w, 0, ffn_conv_w, ffn_conv_b, layer=i, mode="ffn",
                                  casts=casts, **in_tiles)
        od_in_w = cast_w[1] if next_odd else None
        h, cast_next = matmul_residual(h, [f], cast_w[0], tm=256,
                                       casts=[(ffn_w_in, i + 1)] if i + 1 < depth else [])
        ffn_in_w = cast_next[0] if cast_next else None
    return h.reshape(batch, seq, dm)
```

```python
import functools

import jax
import jax.numpy as jnp
from jax import lax
from jax.experimental import pallas as pl
from jax.experimental.pallas import tpu as pltpu

F32 = jnp.float32
BF16 = jnp.bfloat16

HEAD_DIM = 128
NSA_GROUP = 4
N_BRANCH = 3
CMP_BLOCK = 32
CMP_STRIDE = 16
SLC_BLOCK = 64
SLC_SHIFT = SLC_BLOCK.bit_length() - 1
LOG2_E = 1.4426950408889634
QK_AHEAD = 2
N_SLC = 16
N_LOCAL_SLC = 2
WINDOW = 512
FORCE_SCORE = 1e9
GMLP_GROUP_DIM = 128
GMLP_CHUNK = 128
CONV_WIDTH = 3
EPS = 1e-6
NEG = -1e30

LANES = 128
F32_SUBLANES = 8
BF16_SUBLANES = 16
VMEM_LIMIT = 56 * 1024 * 1024


def _params(*sem):
    return pltpu.CompilerParams(dimension_semantics=sem, vmem_limit_bytes=VMEM_LIMIT)


def _dot(a, b):
    return jnp.dot(a, b, preferred_element_type=F32)


def _rms(x, gain):
    return x * lax.rsqrt(jnp.mean(x * x, axis=-1, keepdims=True) + EPS) * gain


def _iota(shape, axis):
    return lax.broadcasted_iota(jnp.int32, shape, axis)


def _resident(shape, index_map):
    return pl.BlockSpec(shape, index_map, pipeline_mode=pl.Buffered(1))


def _cast_row_blocks(rows, n_steps):
    for nb in range(min(n_steps, rows // BF16_SUBLANES), 0, -1):
        if rows % nb == 0 and (rows // nb) % BF16_SUBLANES == 0:
            return nb
    raise ValueError(f"cannot split {rows} rows")


def _call_with_casts(body, *, grid, in_specs, out_spec, out_shape, scratch_shapes, sem, name, args, casts):
    n_steps = 1
    for n in grid:
        n_steps *= n

    def step(*idx):
        s = idx[0]
        for k, n in zip(idx[1:], grid[1:]):
            s = s * n + k
        return s

    n_in, n_cast = len(in_specs), len(casts)
    src_specs, dst_specs, dst_shapes = [], [], []
    for w, layer in casts:
        _, rows, c = w.shape
        nb = _cast_row_blocks(rows, n_steps)
        blk = functools.partial(lambda *idx, nb: jnp.minimum(step(*idx), nb - 1), nb=nb)
        src_specs.append(pl.BlockSpec((None, rows // nb, c),
                                      functools.partial(lambda *idx, blk, layer: (layer, blk(*idx), 0),
                                                        blk=blk, layer=layer)))
        dst_specs.append(pl.BlockSpec((None, rows // nb, c),
                                      functools.partial(lambda *idx, blk: (0, blk(*idx), 0), blk=blk)))
        dst_shapes.append(jax.ShapeDtypeStruct((1, rows, c), BF16))

    def with_casts(*refs):
        ins, src, rest = refs[:n_in], refs[n_in:n_in + n_cast], refs[n_in + n_cast:]
        out, dst, scratch = rest[0], rest[1:1 + n_cast], rest[1 + n_cast:]
        for s_ref, d_ref in zip(src, dst):
            d_ref[...] = s_ref[...].astype(BF16)
        body(*ins, out, *scratch)

    outs = pl.pallas_call(
        with_casts,
        grid=grid,
        in_specs=list(in_specs) + src_specs,
        out_specs=[out_spec] + dst_specs,
        out_shape=[out_shape] + dst_shapes,
        scratch_shapes=scratch_shapes,
        compiler_params=_params(*sem),
        name=name,
    )(*args, *[w for w, _ in casts])
    return outs[0], list(outs[1:])


def _normed_rows(x_ref, g_ref, xn_ref, rows, first):
    if not first:
        return xn_ref[rows, :]
    xc = _rms(x_ref[rows, :], g_ref[...]).astype(BF16)
    xn_ref[rows, :] = xc
    return xc


def _first_then_rest(j, body):
    pl.when(j == 0)(functools.partial(body, True))
    pl.when(j != 0)(functools.partial(body, False))


def _norm_matmul_body(x_ref, g_ref, w_ref, o_ref, xn_ref, *, rc):
    def body(first):
        for c in range(x_ref.shape[0] // rc):
            rows = slice(c * rc, (c + 1) * rc)
            xc = _normed_rows(x_ref, g_ref, xn_ref, rows, first)
            o_ref[rows, :] = _dot(xc, w_ref[...]).astype(o_ref.dtype)

    _first_then_rest(pl.program_id(1), body)


def norm_matmul(x, g, w, *, tm, tn, rc, casts=()):
    t, d = x.shape
    n = w.shape[1]
    assert t % tm == 0 and n % tn == 0 and tm % rc == 0
    return _call_with_casts(
        functools.partial(_norm_matmul_body, rc=rc),
        grid=(t // tm, n // tn),
        in_specs=[pl.BlockSpec((tm, d), lambda i, j: (i, 0)),
                  _resident((1, d), lambda i, j: (0, 0)),
                  (_resident if n == tn else pl.BlockSpec)((d, tn), lambda i, j: (0, j))],
        out_spec=pl.BlockSpec((tm, tn), lambda i, j: (i, j)),
        out_shape=jax.ShapeDtypeStruct((t, n), BF16),
        scratch_shapes=[pltpu.VMEM((tm, d), BF16)],
        sem=("arbitrary", "arbitrary"),
        name="norm_matmul",
        args=(x, g.reshape(1, d), w),
        casts=casts)


def _shift_rows(cur, tail, k):
    out = pltpu.roll(cur, k, axis=0)
    row = _iota(cur.shape, 0)
    for r in range(k):
        src = F32_SUBLANES - k + r
        out = jnp.where(row == r, tail[src:src + 1, :], out)
    return out


def _conv3(cur, tail, cw):
    return (cw[0:1, :] * _shift_rows(cur, tail, 2) + cw[1:2, :] * _shift_rows(cur, tail, 1)
            + cw[2:3, :] * cur)


def _gated_in_body(x_ref, gn_ref, *rest, mode, rc, seq_tiles):
    n_parts = 2 if mode == "ffn" else 3
    w_refs = rest[:n_parts]
    cw_ref, cb_ref, h_ref, xn_ref, tail_ref = rest[n_parts:]
    i, j = pl.program_id(0), pl.program_id(1)
    tm = x_ref.shape[0]

    @pl.when(i % seq_tiles == 0)
    def _():
        tail_ref[j] = jnp.zeros(tail_ref.shape[1:], F32)

    def body(first):
        cw = cw_ref[...]
        tail = tail_ref[j]
        for c in range(tm // rc):
            rows = slice(c * rc, (c + 1) * rc)
            xc = _normed_rows(x_ref, gn_ref, xn_ref, rows, first)
            parts = [_dot(xc, w_ref[...]) for w_ref in w_refs]
            if mode == "ffn":
                g, u = parts
                h = jax.nn.silu(_conv3(g, tail, cw) + cb_ref[...]) * u
                tail = g[rc - F32_SUBLANES:, :]
            else:
                a, cc, dd = parts
                m = cc * dd
                h = a * _conv3(m, tail, cw)
                tail = m[rc - F32_SUBLANES:, :]
            h_ref[rows, :] = h.astype(h_ref.dtype)
        tail_ref[j] = tail

    _first_then_rest(j, body)


def gated_in_proj(x, g, w, w_layer, conv_w, conv_b, *, layer, mode, seq, tm, tn, rc, casts=()):
    t, d = x.shape
    n_parts = 2 if mode == "ffn" else 3
    f = w.shape[2] // n_parts
    assert t % tm == 0 and f % tn == 0 and seq % tm == 0 and tm % rc == 0
    nj = f // tn
    w_spec = _resident if nj == 1 else pl.BlockSpec
    w_specs = [w_spec((None, d, tn), functools.partial(lambda i, j, p: (w_layer, 0, p * nj + j), p=p))
               for p in range(n_parts)]
    return _call_with_casts(
        functools.partial(_gated_in_body, mode=mode, rc=rc, seq_tiles=seq // tm),
        grid=(t // tm, nj),
        in_specs=[pl.BlockSpec((tm, d), lambda i, j: (i, 0)),
                  _resident((None, 1, d), lambda i, j: (layer, 0, 0))] + w_specs + [
                  pl.BlockSpec((None, CONV_WIDTH, tn), lambda i, j: (layer, 0, j)),
                  pl.BlockSpec((None, 1, tn), lambda i, j: (layer, 0, j))],
        out_spec=pl.BlockSpec((tm, tn), lambda i, j: (i, j)),
        out_shape=jax.ShapeDtypeStruct((t, f), BF16),
        scratch_shapes=[pltpu.VMEM((tm, d), BF16), pltpu.VMEM((nj, F32_SUBLANES, tn), F32)],
        sem=("arbitrary", "arbitrary"),
        name=mode + "_in_proj",
        args=(x, g[:, None, :], *([w] * n_parts), conv_w, conv_b[:, None, :]),
        casts=casts)


def _matmul_residual_body(x_ref, *rest):
    n = (len(rest) - 1) // 2
    h_refs, w_refs, o_ref = rest[:n], rest[n:2 * n], rest[2 * n]
    acc = x_ref[...]
    for h_ref, w_ref in zip(h_refs, w_refs):
        acc = acc + _dot(h_ref[...], w_ref[...])
    o_ref[...] = acc


def matmul_residual(x, hs, w, *, tm, casts=()):
    t, d = x.shape
    k = hs[0].shape[1]
    assert all(h.shape == (t, k) for h in hs) and w.shape == (1, k * len(hs), d) and t % tm == 0
    h_specs = [pl.BlockSpec((tm, k), lambda i: (i, 0)) for _ in hs]
    w_specs = [_resident((None, k, d), functools.partial(lambda i, p: (0, p, 0), p=p))
               for p in range(len(hs))]
    return _call_with_casts(
        _matmul_residual_body,
        grid=(t // tm,),
        in_specs=[pl.BlockSpec((tm, d), lambda i: (i, 0))] + h_specs + w_specs,
        out_spec=pl.BlockSpec((tm, d), lambda i: (i, 0)),
        out_shape=jax.ShapeDtypeStruct((t, d), F32),
        scratch_shapes=[],
        sem=("arbitrary",),
        name="out_proj",
        args=(x, *hs, *([w] * len(hs))),
        casts=casts)


def _gmlp_body(u_ref, v_ref, gain_ref, ws_ref, bs_ref, o_ref, *, chunks):
    c, gd = GMLP_CHUNK, GMLP_GROUP_DIM
    causal = _iota((c, c), 1) <= _iota((c, c), 0)
    for g in range(ws_ref.shape[0]):
        cols = slice(g * gd, (g + 1) * gd)
        v = jax.nn.gelu(v_ref[:, cols].astype(F32))
        vc = v - jnp.mean(v, axis=-1, keepdims=True)
        vn = vc * lax.rsqrt(jnp.mean(vc * vc, axis=-1, keepdims=True) + EPS) * gain_ref[g]
        vn = vn.astype(BF16)
        w = jnp.where(causal, ws_ref[g], 0.0).astype(BF16)
        bias = bs_ref[g]
        for ci in range(chunks):
            rows = slice(ci * c, (ci + 1) * c)
            mixed = _dot(w, vn[rows, :]) + bias
            u = jax.nn.gelu(u_ref[rows, cols].astype(F32))
            o_ref[rows, cols] = (u * mixed).astype(o_ref.dtype)


def gmlp(p, gain, ws, bs, *, u_blk, v_blk, tt):
    t = p.shape[0]
    groups, c, _ = ws.shape
    gd = GMLP_GROUP_DIM
    width = groups * gd
    assert t % tt == 0 and tt % c == 0
    full = lambda shape: _resident(shape, lambda i: (0,) * len(shape))
    return pl.pallas_call(
        functools.partial(_gmlp_body, chunks=tt // c),
        grid=(t // tt,),
        in_specs=[pl.BlockSpec((tt, width), lambda i: (i, u_blk)),
                  pl.BlockSpec((tt, width), lambda i: (i, v_blk)),
                  full((groups, 1, gd)), full((groups, c, c)), full((groups, c, 1))],
        out_specs=pl.BlockSpec((tt, width), lambda i: (i, 0)),
        out_shape=jax.ShapeDtypeStruct((t, width), BF16),
        compiler_params=_params("parallel"),
        name="gmlp",
    )(p, p, gain.reshape(groups, 1, gd), ws, bs.reshape(groups, c, 1))


def _compress_body(k_ref, v_ref, pe_ref, kw1_ref, kw2_ref, vw1_ref, vw2_ref, kg_ref, kc_ref, vct_ref, x_ref):
    st = CMP_STRIDE
    n = x_ref.shape[0] // st

    def mlp(src_ref, w1_ref, w2_ref):
        x_ref[...] = src_ref[...].astype(F32)

        def half(part):
            return jnp.concatenate(
                [(x_ref[pl.ds(l, n, stride=st), :] + pe_ref[part * st + l:part * st + l + 1, :]).astype(BF16)
                 for l in range(st)], axis=1)

        a = _dot(half(0), w1_ref[0])
        b = _dot(half(1), w1_ref[1])
        hid = jax.nn.gelu(a + pltpu.roll(b, n - 1, axis=0))
        return _dot(hid.astype(BF16), w2_ref[...])

    kc_ref[...] = _rms(mlp(k_ref, kw1_ref, kw2_ref), kg_ref[...]).astype(kc_ref.dtype)
    vct_ref[...] = mlp(v_ref, vw1_ref, vw2_ref).T.astype(vct_ref.dtype)


def compress(p, pe, kw1, kw2, vw1, vw2, kgain, *, batch, seq, g_kv, k_blk, v_blk):
    d = HEAD_DIM
    st = CMP_STRIDE
    n = seq // st
    hd = st * d
    assert CMP_BLOCK == 2 * st and pe.shape == (CMP_BLOCK, d) and kw1.shape == (2, hd, d)
    full = lambda shape: pl.BlockSpec(shape, lambda i, j: (0,) * len(shape))
    return pl.pallas_call(
        _compress_body,
        grid=(batch, g_kv),
        in_specs=[pl.BlockSpec((seq, d), lambda i, j: (i, k_blk + j)),
                  pl.BlockSpec((seq, d), lambda i, j: (i, v_blk + j)),
                  full((CMP_BLOCK, d)), full((2, hd, d)), full((d, d)),
                  full((2, hd, d)), full((d, d)), full((1, d))],
        out_specs=[pl.BlockSpec((None, None, n, d), lambda i, j: (i, j, 0, 0)),
                   pl.BlockSpec((None, None, d, n), lambda i, j: (i, j, 0, 0))],
        out_shape=[jax.ShapeDtypeStruct((batch, g_kv, n, d), BF16),
                   jax.ShapeDtypeStruct((batch, g_kv, d, n), BF16)],
        scratch_shapes=[pltpu.VMEM((seq, d), F32)],
        compiler_params=_params("parallel", "arbitrary"),
        name="nsa_compress",
    )(p, p, pe, kw1, kw2, vw1, vw2, kgain)


def _col_softmax(s, mask, *, may_be_empty):
    s = jnp.where(mask, s, NEG)
    p = jnp.exp2(s - jnp.max(s, axis=0, keepdims=True))
    if may_be_empty:
        p = jnp.where(mask, p, 0.0)
    return p, jnp.sum(p, axis=0, keepdims=True)


def _nsa_body(q_ref, kc_ref, vct_ref, ov_ref, kv_ref, gl_ref, qg_ref, kg_ref, o_ref,
              ksx_ref, kwn_ref, vst_ref, vwt_ref, m_ref, l_ref, acc_ref, *, tq, tk, n_slc, n_top):
    r_, d = NSA_GROUP, HEAD_DIM
    cols = r_ * tq
    nc = kc_ref.shape[0]
    seq = kv_ref.shape[0]
    g_kv = kv_ref.shape[1] // (4 * d)
    qi = pl.program_id(2)
    q0 = qi * tq
    head = lambda r: slice(r * tq, (r + 1) * tq)

    def stage_kv(g):
        sect = lambda n: slice((n * g_kv + g) * d, (n * g_kv + g + 1) * d)
        ksx_ref[:, 0:d] = _rms(kv_ref[:, sect(0)].astype(F32), kg_ref[1:2, :]).astype(BF16)
        ksx_ref[:, d:2 * d] = (_iota((seq, d), 0) >> SLC_SHIFT == _iota((seq, d), 1)).astype(BF16)
        kwn_ref[...] = _rms(kv_ref[:, sect(2)].astype(F32), kg_ref[2:3, :]).astype(BF16)
        for c in range(seq // tk):
            vst_ref[c] = kv_ref[c * tk:(c + 1) * tk, sect(1)].astype(F32).T.astype(BF16)
        for c in range(seq // tq):
            vwt_ref[c] = kv_ref[c * tq:(c + 1) * tq, sect(3)].astype(F32).T.astype(BF16)

    for g in range(g_kv):
        pl.when((qi == 0) & (pl.program_id(1) == g))(functools.partial(stage_kv, g))

    qraw = q_ref[...].astype(F32)
    qt = jnp.concatenate(
        [(_rms(qraw[:, r * d:(r + 1) * d], qg_ref[...]) * (d ** -0.5 * LOG2_E)).T for r in range(r_)],
        axis=1).astype(BF16)

    sc = _dot(kc_ref[...], qt)
    mask_c = _iota((nc, tq), 0) * CMP_STRIDE + (CMP_BLOCK - 1) <= q0 + _iota((nc, tq), 1)
    o_cmp = []
    psum = None
    for r in range(r_):
        p, den = _col_softmax(sc[:, head(r)], mask_c, may_be_empty=True)
        p = p / jnp.maximum(den, 1e-30)
        psum = p if psum is None else psum + p
        o_cmp.append(_dot(vct_ref[...], p.astype(BF16)))

    overlap = ov_ref[...]
    hi = psum.astype(BF16)
    r1 = psum - hi.astype(F32)
    lo = r1.astype(BF16)
    lo2 = (r1 - lo.astype(F32)).astype(BF16)
    imp = _dot(overlap, hi) + _dot(overlap, lo) + _dot(overlap, lo2)

    j_s = _iota((n_slc, tq), 0)
    pos_s = q0 + _iota((n_slc, tq), 1)
    dlt = (pos_s >> SLC_SHIFT) - j_s
    forced = (j_s == 0) | ((dlt >= 0) & (dlt < N_LOCAL_SLC))
    val = jnp.where(dlt >= 0, imp, NEG)
    val = jnp.where(forced, FORCE_SCORE, val)
    sub = F32_SUBLANES
    groups = [val[g0:g0 + sub, :] for g0 in range(0, n_slc, sub)]
    cnts = [jnp.zeros((sub, tq), jnp.int32) for _ in groups]
    j_g = _iota((sub, tq), 0)
    for i in range(n_slc):
        vi = val[i:i + 1, :]
        for gi, vg in enumerate(groups):
            g0 = gi * sub
            if g0 > i:
                beats = vi >= vg
            elif g0 + sub - 1 < i:
                beats = vi > vg
            else:
                beats = (vi > vg) | ((vi == vg) & (j_g + g0 > i))
            cnts[gi] = cnts[gi] + beats.astype(jnp.int32)
    cnt = jnp.concatenate(cnts, axis=0)
    bias = jnp.where(cnt < n_top, 0.0, NEG)
    if n_slc < d:
        bias = jnp.concatenate([bias, jnp.zeros((d - n_slc, tq), F32)], axis=0)
    qx = jnp.concatenate([qt, jnp.concatenate([bias.astype(BF16)] * r_, axis=1)], axis=0)

    m_ref[...] = jnp.full((1, cols), NEG, F32)
    l_ref[...] = jnp.zeros((1, cols), F32)
    acc_ref[...] = jnp.zeros((d, cols), F32)

    def scores(kt):
        k0 = pl.multiple_of(kt * tk, tk)
        return _dot(ksx_ref[pl.ds(k0, tk), :], qx)

    def fold(kt, r, s, causal):
        if causal is not None:
            s = jnp.where(causal >= kt * tk, s, NEG)
        m_old = m_ref[:, head(r)]
        m_new = jnp.maximum(m_old, jnp.max(s, axis=0, keepdims=True))
        alpha = jnp.exp2(m_old - m_new)
        p = jnp.exp2(s - m_new)
        l_ref[:, head(r)] = alpha * l_ref[:, head(r)] + jnp.sum(p, axis=0, keepdims=True)
        acc_ref[:, head(r)] = alpha * acc_ref[:, head(r)] + _dot(vst_ref[kt], p.astype(BF16))
        m_ref[:, head(r)] = m_new

    def slc_tiles(kts, causal):
        s_all = [scores(kt) for kt in kts]
        for kt, s in zip(kts, s_all):
            for r in range(r_):
                fold(kt, r, s[:, head(r)], causal)

    n_quads = q0 // (4 * tk)

    def quad(i, carry):
        slc_tiles([4 * i + dd for dd in range(4)], None)
        return carry

    lax.fori_loop(0, n_quads, quad, 0)
    causal_gap = q0 + _iota((tk, tq), 1) - _iota((tk, tq), 0)
    n_tiles = (q0 + tq + tk - 1) // tk

    def causal_pair(i, carry):
        slc_tiles([4 * n_quads + 2 * i, 4 * n_quads + 2 * i + 1], causal_gap)
        return carry

    lax.fori_loop(0, (n_tiles - 4 * n_quads + 1) // 2, causal_pair, 0)

    nw = WINDOW // tq + 1
    wc = jnp.maximum(qi - WINDOW // tq, 0)
    w0 = pl.multiple_of(wc * tq, tq)
    sw = _dot(kwn_ref[pl.ds(w0, nw * tq), :], qt)
    gap_w = q0 + _iota((nw * tq, tq), 1) - (w0 + _iota((nw * tq, tq), 0))
    mask_w = (gap_w >= 0) & (gap_w < WINDOW)
    o_win = []
    for r in range(r_):
        p, den = _col_softmax(sw[:, head(r)], mask_w, may_be_empty=False)
        p = p.astype(BF16)
        o = _dot(vwt_ref[wc], p[0:tq, :])
        for c in range(1, nw):
            o = o + _dot(vwt_ref[wc + c], p[c * tq:(c + 1) * tq, :])
        o_win.append(o / den)

    gate = jax.nn.sigmoid(gl_ref[...].astype(F32)).T
    for r in range(r_):
        c = N_BRANCH * r
        o_slc = acc_ref[:, head(r)] / l_ref[:, head(r)]
        o = (gate[c:c + 1, :] * o_cmp[r] + gate[c + 1:c + 2, :] * o_slc
             + gate[c + 2:c + 3, :] * o_win[r])
        o_ref[:, r * d:(r + 1) * d] = o.T.astype(o_ref.dtype)


def nsa_attention(p, kc, vct, q_gain, k_gain, *, batch, seq, col, tq, tk, casts=()):
    b, g, nc, d = kc.shape
    r_ = NSA_GROUP
    nq = seq // tq
    n_slc = seq // SLC_BLOCK
    assert seq % tq == 0 and seq % tk == 0 and WINDOW % tq == 0 and WINDOW + tq <= seq
    assert ((tq == tk and nq % 2 == 0) or tq == 2 * tk) and n_slc <= d
    cols = r_ * tq
    cmp_start = jnp.arange(nc)[None, :] * CMP_STRIDE
    slc_start = jnp.arange(n_slc)[:, None] * SLC_BLOCK
    overlap = ((cmp_start < slc_start + SLC_BLOCK) & (cmp_start + CMP_BLOCK > slc_start)).astype(BF16)
    kv4 = 4 * g * d
    return _call_with_casts(
        functools.partial(_nsa_body, tq=tq, tk=tk, n_slc=n_slc, n_top=min(N_SLC, n_slc)),
        grid=(batch, g, nq),
        in_specs=[pl.BlockSpec((tq, r_ * d), lambda bi, gi, qi: (bi * nq + qi, col["q"] // r_ + gi)),
                  pl.BlockSpec((None, None, nc, d), lambda bi, gi, qi: (bi, gi, 0, 0)),
                  pl.BlockSpec((None, None, d, nc), lambda bi, gi, qi: (bi, gi, 0, 0)),
                  _resident((n_slc, nc), lambda bi, gi, qi: (0, 0)),
                  pl.BlockSpec((seq, kv4), lambda bi, gi, qi: (bi, col["kv4"])),
                  pl.BlockSpec((tq, LANES), lambda bi, gi, qi: (bi * nq + qi, col["gate"] + gi)),
                  pl.BlockSpec((1, d), lambda bi, gi, qi: (0, 0)),
                  pl.BlockSpec((N_BRANCH, d), lambda bi, gi, qi: (0, 0))],
        out_spec=pl.BlockSpec((tq, r_ * d), lambda bi, gi, qi: (bi * nq + qi, gi)),
        out_shape=jax.ShapeDtypeStruct((batch * seq, g * r_ * d), BF16),
        scratch_shapes=[pltpu.VMEM((seq, 2 * d), BF16), pltpu.VMEM((seq, d), BF16),
                        pltpu.VMEM((seq // tk, d, tk), BF16), pltpu.VMEM((seq // tq, d, tq), BF16),
                        pltpu.VMEM((1, cols), F32), pltpu.VMEM((1, cols), F32),
                        pltpu.VMEM((d, cols), F32)],
        sem=("arbitrary", "arbitrary", "arbitrary"),
        name="nsa_attention",
        args=(p, kc, vct, overlap, p, p, q_gain.reshape(1, d), k_gain),
        casts=casts)


def _pad_cols(w, n):
    return jnp.pad(w, ((0, 0), (0, n - w.shape[1])))


def _even_mixer(x, batch, seq, norm_g, w_in, q_gain, k_gain, cmp_pe, kw1, kw2, vw1, vw2,
                gmlp_norm, gmlp_ws, gmlp_b, casts, nsa_casts):
    d = HEAD_DIM
    dm = x.shape[1]
    gmlp_w = gmlp_ws.shape[0] * GMLP_GROUP_DIM
    qw = dm - gmlp_w
    g_kv = max(1, qw // d // NSA_GROUP)
    kv = g_kv * d
    n_kv = 2 * N_BRANCH
    n_gate = N_BRANCH * qw // d
    assert w_in.shape[1] == qw + n_kv * kv + n_gate + 2 * gmlp_w
    o_gl = qw + n_kv * kv
    o_u = o_gl + n_gate
    per_g = n_gate // g_kv
    gate_cols = [_pad_cols(w_in[:, o_gl + gi * per_g: o_gl + (gi + 1) * per_g], LANES) for gi in range(g_kv)]
    w_nsa = jnp.concatenate([w_in[:, :qw], w_in[:, qw + 2 * kv:o_gl], w_in[:, qw:qw + 2 * kv]] + gate_cols, axis=1)
    u_blk = -(-w_nsa.shape[1] // gmlp_w)
    w_cat = jnp.concatenate([_pad_cols(w_nsa, u_blk * gmlp_w), w_in[:, o_u:]], axis=1).astype(BF16)
    tn = w_cat.shape[1]
    assert qw % (4 * kv) == 0
    col = {"q": 0, "kv4": qw // (4 * kv), "kc": (qw + 4 * kv) // d, "vc": (qw + 5 * kv) // d, "gate": o_gl // d}

    p, cast_w = norm_matmul(x, norm_g, w_cat, tm=512, tn=tn, rc=256, casts=casts)

    hd = CMP_STRIDE * d
    kc, vct = compress(p, cmp_pe, kw1.reshape(2, hd, d).astype(BF16), kw2.astype(BF16),
                       vw1.reshape(2, hd, d).astype(BF16), vw2.astype(BF16), k_gain[0:1],
                       batch=batch, seq=seq, g_kv=g_kv, k_blk=col["kc"], v_blk=col["vc"])
    oa, nsa_cast_w = nsa_attention(p, kc, vct, q_gain, k_gain, batch=batch, seq=seq, col=col, tq=512, tk=256,
                                   casts=nsa_casts)
    ob = gmlp(p, gmlp_norm, gmlp_ws, gmlp_b, u_blk=u_blk, v_blk=u_blk + 1, tt=512)
    return oa, ob, cast_w, nsa_cast_w


def kernel(x, ev_norm, ev_w_in, ev_q_gain, ev_k_gain, ev_cmp_pe, ev_cmp_k_w1, ev_cmp_k_w2, ev_cmp_v_w1, ev_cmp_v_w2, ev_gmlp_norm, ev_gmlp_ws, ev_gmlp_b, ev_w_out, od_norm, od_w_in, od_conv_w, od_w_out, ffn_norm, ffn_w_in, ffn_conv_w, ffn_conv_b, ffn_w_down):
    batch, seq, dm = x.shape
    depth = ffn_norm.shape[0]
    no_bias = jnp.zeros((od_conv_w.shape[0], od_conv_w.shape[2]), F32)
    in_tiles = dict(seq=seq, tm=1024, tn=512, rc=256)
    sconv_tiles = dict(seq=seq, tm=512, tn=od_w_in.shape[2] // 3, rc=256)
    h = x.reshape(batch * seq, dm)
    ffn_in_w = od_in_w = None
    for i in range(depth):
        j = i // 2
        if i % 2 == 0:
            oa, ob, (out_w,), cast_w = _even_mixer(
                h, batch, seq, ev_norm[j], ev_w_in[j], ev_q_gain[j], ev_k_gain[j], ev_cmp_pe[j],
                ev_cmp_k_w1[j], ev_cmp_k_w2[j], ev_cmp_v_w1[j], ev_cmp_v_w2[j], ev_gmlp_norm[j],
                ev_gmlp_ws[j], ev_gmlp_b[j], [(ev_w_out, j)], [(ffn_w_in, i)] if ffn_in_w is None else [])
            ffn_in_w = cast_w[0] if cast_w else ffn_in_w
            h, _ = matmul_residual(h, [oa, ob], out_w, tm=512)
        else:
            m, (out_w,) = gated_in_proj(h, od_norm, od_in_w, 0, od_conv_w, no_bias, layer=j, mode="sconv",
                                        casts=[(od_w_out, j)], **sconv_tiles)
            h, _ = matmul_residual(h, [m], out_w, tm=512)
        next_odd = i + 1 < depth and (i + 1) % 2 == 1
        casts = [(ffn_w_down, i)] + ([(od_w_in, (i + 1) // 2)] if next_odd else [])
        f, cast_w = gated_in_proj(h, ffn_norm, ffn_in_w, 0, ffn_conv_w, ffn_conv_b, layer=i, mode="ffn",
                                  casts=casts, **in_tiles)
        od_in_w = cast_w[1] if next_odd else None
        h, cast_next = matmul_residual(h, [f], cast_w[0], tm=256,
                                       casts=[(ffn_w_in, i + 1)] if i + 1 < depth else [])
        ffn_in_w = cast_next[0] if cast_next else None
    return h.reshape(batch, seq, dm)
```

```python
import functools

import jax
import jax.numpy as jnp
from jax import lax
from jax.experimental import pallas as pl
from jax.experimental.pallas import tpu as pltpu

F32 = jnp.float32
BF16 = jnp.bfloat16

HEAD_DIM = 128
NSA_GROUP = 4
N_BRANCH = 3
CMP_BLOCK = 32
CMP_STRIDE = 16
SLC_BLOCK = 64
SLC_SHIFT = SLC_BLOCK.bit_length() - 1
LOG2_E = 1.4426950408889634
N_SLC = 16
N_LOCAL_SLC = 2
WINDOW = 512
FORCE_SCORE = 1e9
GMLP_GROUP_DIM = 128
GMLP_CHUNK = 128
CONV_WIDTH = 3
EPS = 1e-6
NEG = -1e30

LANES = 128
F32_SUBLANES = 8
BF16_SUBLANES = 16
SUM_ROWS = BF16_SUBLANES
VMEM_LIMIT = 56 * 1024 * 1024


def _params(*sem):
    return pltpu.CompilerParams(dimension_semantics=sem, vmem_limit_bytes=VMEM_LIMIT)


def _dot(a, b):
    return jnp.dot(a, b, preferred_element_type=F32)


def _rms(x, gain):
    return x * lax.rsqrt(jnp.mean(x * x, axis=-1, keepdims=True) + EPS) * gain


def _iota(shape, axis):
    return lax.broadcasted_iota(jnp.int32, shape, axis)


def _resident(shape, index_map):
    return pl.BlockSpec(shape, index_map, pipeline_mode=pl.Buffered(1))


def _cast_row_blocks(rows, n_steps):
    for nb in range(min(n_steps, rows // BF16_SUBLANES), 0, -1):
        if rows % nb == 0 and (rows // nb) % BF16_SUBLANES == 0:
            return nb
    raise ValueError(f"cannot split {rows} rows")


def _call_with_casts(body, *, grid, in_specs, out_spec, out_shape, scratch_shapes, sem, name, args, casts):
    n_steps = 1
    for n in grid:
        n_steps *= n

    def step(*idx):
        s = idx[0]
        for k, n in zip(idx[1:], grid[1:]):
            s = s * n + k
        return s

    n_in, n_cast = len(in_specs), len(casts)
    src_specs, dst_specs, dst_shapes = [], [], []
    for w, layer in casts:
        _, rows, c = w.shape
        nb = _cast_row_blocks(rows, n_steps)
        blk = functools.partial(lambda *idx, nb: jnp.minimum(step(*idx), nb - 1), nb=nb)
        src_specs.append(pl.BlockSpec((None, rows // nb, c),
                                      functools.partial(lambda *idx, blk, layer: (layer, blk(*idx), 0),
                                                        blk=blk, layer=layer)))
        dst_specs.append(pl.BlockSpec((None, rows // nb, c),
                                      functools.partial(lambda *idx, blk: (0, blk(*idx), 0), blk=blk)))
        dst_shapes.append(jax.ShapeDtypeStruct((1, rows, c), BF16))

    def with_casts(*refs):
        ins, src, rest = refs[:n_in], refs[n_in:n_in + n_cast], refs[n_in + n_cast:]
        out, dst, scratch = rest[0], rest[1:1 + n_cast], rest[1 + n_cast:]
        for s_ref, d_ref in zip(src, dst):
            d_ref[...] = s_ref[...].astype(BF16)
        body(*ins, out, *scratch)

    outs = pl.pallas_call(
        with_casts,
        grid=grid,
        in_specs=list(in_specs) + src_specs,
        out_specs=[out_spec] + dst_specs,
        out_shape=[out_shape] + dst_shapes,
        scratch_shapes=scratch_shapes,
        compiler_params=_params(*sem),
        name=name,
    )(*args, *[w for w, _ in casts])
    return outs[0], list(outs[1:])


def _normed_rows(x_ref, g_ref, xn_ref, rows, first):
    if not first:
        return xn_ref[rows, :]
    xc = _rms(x_ref[rows, :], g_ref[...]).astype(BF16)
    xn_ref[rows, :] = xc
    return xc


def _first_then_rest(j, body):
    pl.when(j == 0)(functools.partial(body, True))
    pl.when(j != 0)(functools.partial(body, False))


def _norm_matmul_body(x_ref, g_ref, w_ref, o_ref, xn_ref, *, rc):
    def body(first):
        for c in range(x_ref.shape[0] // rc):
            rows = slice(c * rc, (c + 1) * rc)
            xc = _normed_rows(x_ref, g_ref, xn_ref, rows, first)
            o_ref[rows, :] = _dot(xc, w_ref[...]).astype(o_ref.dtype)

    _first_then_rest(pl.program_id(1), body)


def norm_matmul(x, g, w, *, tm, tn, rc, casts=()):
    t, d = x.shape
    n = w.shape[1]
    assert t % tm == 0 and n % tn == 0 and tm % rc == 0
    return _call_with_casts(
        functools.partial(_norm_matmul_body, rc=rc),
        grid=(t // tm, n // tn),
        in_specs=[pl.BlockSpec((tm, d), lambda i, j: (i, 0)),
                  _resident((1, d), lambda i, j: (0, 0)),
                  (_resident if n == tn else pl.BlockSpec)((d, tn), lambda i, j: (0, j))],
        out_spec=pl.BlockSpec((tm, tn), lambda i, j: (i, j)),
        out_shape=jax.ShapeDtypeStruct((t, n), BF16),
        scratch_shapes=[pltpu.VMEM((tm, d), BF16)],
        sem=("arbitrary", "arbitrary"),
        name="norm_matmul",
        args=(x, g.reshape(1, d), w),
        casts=casts)


def _shift_rows(cur, tail, k):
    out = pltpu.roll(cur, k, axis=0)
    row = _iota(cur.shape, 0)
    for r in range(k):
        src = F32_SUBLANES - k + r
        out = jnp.where(row == r, tail[src:src + 1, :], out)
    return out


def _conv3(cur, tail, cw):
    return (cw[0:1, :] * _shift_rows(cur, tail, 2) + cw[1:2, :] * _shift_rows(cur, tail, 1)
            + cw[2:3, :] * cur)


def _gated_in_body(x_ref, gn_ref, *rest, mode, rc, seq_tiles):
    n_parts = 2 if mode == "ffn" else 3
    w_refs = rest[:n_parts]
    cw_ref, cb_ref, h_ref, xn_ref, tail_ref = rest[n_parts:]
    i, j = pl.program_id(0), pl.program_id(1)
    tm = x_ref.shape[0]

    @pl.when(i % seq_tiles == 0)
    def _():
        tail_ref[j] = jnp.zeros(tail_ref.shape[1:], F32)

    def body(first):
        cw = cw_ref[...]
        tail = tail_ref[j]
        for c in range(tm // rc):
            rows = slice(c * rc, (c + 1) * rc)
            xc = _normed_rows(x_ref, gn_ref, xn_ref, rows, first)
            parts = [_dot(xc, w_ref[...]) for w_ref in w_refs]
            if mode == "ffn":
                g, u = parts
                h = jax.nn.silu(_conv3(g, tail, cw) + cb_ref[...]) * u
                tail = g[rc - F32_SUBLANES:, :]
            else:
                a, cc, dd = parts
                m = cc * dd
                h = a * _conv3(m, tail, cw)
                tail = m[rc - F32_SUBLANES:, :]
            h_ref[rows, :] = h.astype(h_ref.dtype)
        tail_ref[j] = tail

    _first_then_rest(j, body)


def gated_in_proj(x, g, w, w_layer, conv_w, conv_b, *, layer, mode, seq, tm, tn, rc, casts=()):
    t, d = x.shape
    n_parts = 2 if mode == "ffn" else 3
    f = w.shape[2] // n_parts
    assert t % tm == 0 and f % tn == 0 and seq % tm == 0 and tm % rc == 0
    nj = f // tn
    w_spec = _resident if nj == 1 else pl.BlockSpec
    w_specs = [w_spec((None, d, tn), functools.partial(lambda i, j, p: (w_layer, 0, p * nj + j), p=p))
               for p in range(n_parts)]
    return _call_with_casts(
        functools.partial(_gated_in_body, mode=mode, rc=rc, seq_tiles=seq // tm),
        grid=(t // tm, nj),
        in_specs=[pl.BlockSpec((tm, d), lambda i, j: (i, 0)),
                  _resident((None, 1, d), lambda i, j: (layer, 0, 0))] + w_specs + [
                  pl.BlockSpec((None, CONV_WIDTH, tn), lambda i, j: (layer, 0, j)),
                  pl.BlockSpec((None, 1, tn), lambda i, j: (layer, 0, j))],
        out_spec=pl.BlockSpec((tm, tn), lambda i, j: (i, j)),
        out_shape=jax.ShapeDtypeStruct((t, f), BF16),
        scratch_shapes=[pltpu.VMEM((tm, d), BF16), pltpu.VMEM((nj, F32_SUBLANES, tn), F32)],
        sem=("arbitrary", "arbitrary"),
        name=mode + "_in_proj",
        args=(x, g[:, None, :], *([w] * n_parts), conv_w, conv_b[:, None, :]),
        casts=casts)


def _matmul_residual_body(x_ref, *rest):
    n = (len(rest) - 1) // 2
    h_refs, w_refs, o_ref = rest[:n], rest[n:2 * n], rest[2 * n]
    acc = x_ref[...]
    for h_ref, w_ref in zip(h_refs, w_refs):
        acc = acc + _dot(h_ref[...], w_ref[...])
    o_ref[...] = acc


def matmul_residual(x, hs, w, *, tm, casts=()):
    t, d = x.shape
    k = hs[0].shape[1]
    assert all(h.shape == (t, k) for h in hs) and w.shape == (1, k * len(hs), d) and t % tm == 0
    h_specs = [pl.BlockSpec((tm, k), lambda i: (i, 0)) for _ in hs]
    w_specs = [_resident((None, k, d), functools.partial(lambda i, p: (0, p, 0), p=p))
               for p in range(len(hs))]
    return _call_with_casts(
        _matmul_residual_body,
        grid=(t // tm,),
        in_specs=[pl.BlockSpec((tm, d), lambda i: (i, 0))] + h_specs + w_specs,
        out_spec=pl.BlockSpec((tm, d), lambda i: (i, 0)),
        out_shape=jax.ShapeDtypeStruct((t, d), F32),
        scratch_shapes=[],
        sem=("arbitrary",),
        name="out_proj",
        args=(x, *hs, *([w] * len(hs))),
        casts=casts)


def _gmlp_body(u_ref, v_ref, gain_ref, ws_ref, bs_ref, o_ref, *, chunks):
    c, gd = GMLP_CHUNK, GMLP_GROUP_DIM
    causal = _iota((c, c), 1) <= _iota((c, c), 0)
    for g in range(ws_ref.shape[0]):
        cols = slice(g * gd, (g + 1) * gd)
        v = jax.nn.gelu(v_ref[:, cols].astype(F32))
        vc = v - jnp.mean(v, axis=-1, keepdims=True)
        vn = vc * lax.rsqrt(jnp.mean(vc * vc, axis=-1, keepdims=True) + EPS) * gain_ref[g]
        vn = vn.astype(BF16)
        w = jnp.where(causal, ws_ref[g], 0.0).astype(BF16)
        bias = bs_ref[g]
        for ci in range(chunks):
            rows = slice(ci * c, (ci + 1) * c)
            mixed = _dot(w, vn[rows, :]) + bias
            u = jax.nn.gelu(u_ref[rows, cols].astype(F32))
            o_ref[rows, cols] = (u * mixed).astype(o_ref.dtype)


def gmlp(p, gain, ws, bs, *, u_blk, v_blk, tt):
    t = p.shape[0]
    groups, c, _ = ws.shape
    gd = GMLP_GROUP_DIM
    width = groups * gd
    assert t % tt == 0 and tt % c == 0
    full = lambda shape: _resident(shape, lambda i: (0,) * len(shape))
    return pl.pallas_call(
        functools.partial(_gmlp_body, chunks=tt // c),
        grid=(t // tt,),
        in_specs=[pl.BlockSpec((tt, width), lambda i: (i, u_blk)),
                  pl.BlockSpec((tt, width), lambda i: (i, v_blk)),
                  full((groups, 1, gd)), full((groups, c, c)), full((groups, c, 1))],
        out_specs=pl.BlockSpec((tt, width), lambda i: (i, 0)),
        out_shape=jax.ShapeDtypeStruct((t, width), BF16),
        compiler_params=_params("parallel"),
        name="gmlp",
    )(p, p, gain.reshape(groups, 1, gd), ws, bs.reshape(groups, c, 1))


def _compress_body(k_ref, v_ref, pe_ref, kw1_ref, kw2_ref, vw1_ref, vw2_ref, kg_ref, kc_ref, vct_ref, x_ref):
    st = CMP_STRIDE
    n = x_ref.shape[0] // st

    def mlp(src_ref, w1_ref, w2_ref):
        x_ref[...] = src_ref[...].astype(F32)

        def half(part):
            return jnp.concatenate(
                [(x_ref[pl.ds(l, n, stride=st), :] + pe_ref[part * st + l:part * st + l + 1, :]).astype(BF16)
                 for l in range(st)], axis=1)

        a = _dot(half(0), w1_ref[0])
        b = _dot(half(1), w1_ref[1])
        hid = jax.nn.gelu(a + pltpu.roll(b, n - 1, axis=0))
        return _dot(hid.astype(BF16), w2_ref[...])

    kc_ref[...] = _rms(mlp(k_ref, kw1_ref, kw2_ref), kg_ref[...]).astype(kc_ref.dtype)
    vct_ref[...] = mlp(v_ref, vw1_ref, vw2_ref).T.astype(vct_ref.dtype)


def compress(p, pe, kw1, kw2, vw1, vw2, kgain, *, batch, seq, g_kv, k_blk, v_blk):
    d = HEAD_DIM
    st = CMP_STRIDE
    n = seq // st
    hd = st * d
    assert CMP_BLOCK == 2 * st and pe.shape == (CMP_BLOCK, d) and kw1.shape == (2, hd, d)
    full = lambda shape: pl.BlockSpec(shape, lambda i, j: (0,) * len(shape))
    return pl.pallas_call(
        _compress_body,
        grid=(batch, g_kv),
        in_specs=[pl.BlockSpec((seq, d), lambda i, j: (i, k_blk + j)),
                  pl.BlockSpec((seq, d), lambda i, j: (i, v_blk + j)),
                  full((CMP_BLOCK, d)), full((2, hd, d)), full((d, d)),
                  full((2, hd, d)), full((d, d)), full((1, d))],
        out_specs=[pl.BlockSpec((None, None, n, d), lambda i, j: (i, j, 0, 0)),
                   pl.BlockSpec((None, None, d, n), lambda i, j: (i, j, 0, 0))],
        out_shape=[jax.ShapeDtypeStruct((batch, g_kv, n, d), BF16),
                   jax.ShapeDtypeStruct((batch, g_kv, d, n), BF16)],
        scratch_shapes=[pltpu.VMEM((seq, d), F32)],
        compiler_params=_params("parallel", "arbitrary"),
        name="nsa_compress",
    )(p, p, pe, kw1, kw2, vw1, vw2, kgain)


def _col_softmax(s, mask, *, may_be_empty, with_sum=True):
    s = jnp.where(mask, s, NEG)
    p = jnp.exp2(s - jnp.max(s, axis=0, keepdims=True))
    if may_be_empty:
        p = jnp.where(mask, p, 0.0)
    return (p, jnp.sum(p, axis=0, keepdims=True)) if with_sum else p


def _nsa_body(q_ref, kc_ref, vct_ref, ov_ref, kv_ref, gl_ref, qg_ref, kg_ref, o_ref,
              ksx_ref, kwn_ref, vst_ref, vwt_ref, m_ref, acc_ref, *, tq, tk, n_slc, n_top):
    r_, d = NSA_GROUP, HEAD_DIM
    cols = r_ * tq
    nc = kc_ref.shape[0]
    seq = kv_ref.shape[0]
    g_kv = kv_ref.shape[1] // (4 * d)
    qi = pl.program_id(2)
    q0 = qi * tq
    head = lambda r: slice(r * tq, (r + 1) * tq)

    def stage_kv(g):
        sect = lambda n: slice((n * g_kv + g) * d, (n * g_kv + g + 1) * d)
        ksx_ref[:, 0:d] = _rms(kv_ref[:, sect(0)].astype(F32), kg_ref[1:2, :]).astype(BF16)
        ksx_ref[:, d:2 * d] = (_iota((seq, d), 0) >> SLC_SHIFT == _iota((seq, d), 1)).astype(BF16)
        kwn_ref[...] = _rms(kv_ref[:, sect(2)].astype(F32), kg_ref[2:3, :]).astype(BF16)
        for c in range(seq // tk):
            vst_ref[c, 0:d, :] = kv_ref[c * tk:(c + 1) * tk, sect(1)].astype(F32).T.astype(BF16)
            vst_ref[c, d:, :] = jnp.ones((SUM_ROWS, tk), BF16)
        for c in range(seq // tq):
            vwt_ref[c, 0:d, :] = kv_ref[c * tq:(c + 1) * tq, sect(3)].astype(F32).T.astype(BF16)
            vwt_ref[c, d:, :] = jnp.ones((SUM_ROWS, tq), BF16)

    for g in range(g_kv):
        pl.when((qi == 0) & (pl.program_id(1) == g))(functools.partial(stage_kv, g))

    qraw = q_ref[...].astype(F32)
    qt = jnp.concatenate(
        [(_rms(qraw[:, r * d:(r + 1) * d], qg_ref[...]) * (d ** -0.5 * LOG2_E)).T for r in range(r_)],
        axis=1).astype(BF16)

    sc = _dot(kc_ref[...], qt)
    mask_c = _iota((nc, tq), 0) * CMP_STRIDE + (CMP_BLOCK - 1) <= q0 + _iota((nc, tq), 1)
    o_cmp = []
    psum = None
    for r in range(r_):
        p, den = _col_softmax(sc[:, head(r)], mask_c, may_be_empty=True)
        p = p / jnp.maximum(den, 1e-30)
        psum = p if psum is None else psum + p
        o_cmp.append(_dot(vct_ref[...], p.astype(BF16)))

    overlap = ov_ref[...]
    hi = psum.astype(BF16)
    r1 = psum - hi.astype(F32)
    lo = r1.astype(BF16)
    lo2 = (r1 - lo.astype(F32)).astype(BF16)
    imp = _dot(overlap, hi) + _dot(overlap, lo) + _dot(overlap, lo2)

    j_s = _iota((n_slc, tq), 0)
    pos_s = q0 + _iota((n_slc, tq), 1)
    dlt = (pos_s >> SLC_SHIFT) - j_s
    forced = (j_s == 0) | ((dlt >= 0) & (dlt < N_LOCAL_SLC))
    val = jnp.where(dlt >= 0, imp, NEG)
    val = jnp.where(forced, FORCE_SCORE, val)
    sub = F32_SUBLANES
    groups = [val[g0:g0 + sub, :] for g0 in range(0, n_slc, sub)]
    cnts = [jnp.zeros((sub, tq), jnp.int32) for _ in groups]
    j_g = _iota((sub, tq), 0)
    for i in range(n_slc):
        vi = val[i:i + 1, :]
        for gi, vg in enumerate(groups):
            g0 = gi * sub
            if g0 > i:
                beats = vi >= vg
            elif g0 + sub - 1 < i:
                beats = vi > vg
            else:
                beats = (vi > vg) | ((vi == vg) & (j_g + g0 > i))
            cnts[gi] = cnts[gi] + beats.astype(jnp.int32)
    cnt = jnp.concatenate(cnts, axis=0)
    bias = jnp.where(cnt < n_top, 0.0, NEG)
    if n_slc < d:
        bias = jnp.concatenate([bias, jnp.zeros((d - n_slc, tq), F32)], axis=0)
    qx = jnp.concatenate([qt, jnp.concatenate([bias.astype(BF16)] * r_, axis=1)], axis=0)

    m_ref[...] = jnp.full((1, cols), NEG, F32)
    acc_ref[...] = jnp.zeros((d + SUM_ROWS, cols), F32)

    def scores(kt):
        k0 = pl.multiple_of(kt * tk, tk)
        return _dot(ksx_ref[pl.ds(k0, tk), :], qx)

    def fold(kt, r, s, causal):
        if causal is not None:
            s = jnp.where(causal >= kt * tk, s, NEG)
        m_old = m_ref[:, head(r)]
        m_new = jnp.maximum(m_old, jnp.max(s, axis=0, keepdims=True))
        alpha = jnp.exp2(m_old - m_new)
        p = jnp.exp2(s - m_new)
        acc_ref[:, head(r)] = alpha * acc_ref[:, head(r)] + _dot(vst_ref[kt], p.astype(BF16))
        m_ref[:, head(r)] = m_new

    def slc_tiles(kts, causal):
        s_all = [scores(kt) for kt in kts]
        for kt, s in zip(kts, s_all):
            for r in range(r_):
                fold(kt, r, s[:, head(r)], causal)

    n_quads = q0 // (4 * tk)

    def quad(i, carry):
        slc_tiles([4 * i + dd for dd in range(4)], None)
        return carry

    lax.fori_loop(0, n_quads, quad, 0)
    causal_gap = q0 + _iota((tk, tq), 1) - _iota((tk, tq), 0)
    n_tiles = (q0 + tq + tk - 1) // tk

    def causal_pair(i, carry):
        slc_tiles([4 * n_quads + 2 * i, 4 * n_quads + 2 * i + 1], causal_gap)
        return carry

    lax.fori_loop(0, (n_tiles - 4 * n_quads + 1) // 2, causal_pair, 0)

    nw = WINDOW // tq + 1
    wc = jnp.maximum(qi - WINDOW // tq, 0)
    w0 = pl.multiple_of(wc * tq, tq)
    sw = _dot(kwn_ref[pl.ds(w0, nw * tq), :], qt)
    gap_w = q0 + _iota((nw * tq, tq), 1) - (w0 + _iota((nw * tq, tq), 0))
    mask_w = (gap_w >= 0) & (gap_w < WINDOW)
    o_win = []
    for r in range(r_):
        p = _col_softmax(sw[:, head(r)], mask_w, may_be_empty=False, with_sum=False).astype(BF16)
        o = _dot(vwt_ref[wc], p[0:tq, :])
        for c in range(1, nw):
            o = o + _dot(vwt_ref[wc + c], p[c * tq:(c + 1) * tq, :])
        o_win.append(o[0:d, :] / o[d:d + 1, :])

    gate = jax.nn.sigmoid(gl_ref[...].astype(F32)).T
    for r in range(r_):
        c = N_BRANCH * r
        o_slc = acc_ref[0:d, head(r)] / acc_ref[d:d + 1, head(r)]
        o = (gate[c:c + 1, :] * o_cmp[r] + gate[c + 1:c + 2, :] * o_slc
             + gate[c + 2:c + 3, :] * o_win[r])
        o_ref[:, r * d:(r + 1) * d] = o.T.astype(o_ref.dtype)


def nsa_attention(p, kc, vct, q_gain, k_gain, *, batch, seq, col, tq, tk, casts=()):
    b, g, nc, d = kc.shape
    r_ = NSA_GROUP
    nq = seq // tq
    n_slc = seq // SLC_BLOCK
    assert seq % tq == 0 and seq % tk == 0 and WINDOW % tq == 0 and WINDOW + tq <= seq
    assert ((tq == tk and nq % 2 == 0) or tq == 2 * tk) and n_slc <= d
    cols = r_ * tq
    cmp_start = jnp.arange(nc)[None, :] * CMP_STRIDE
    slc_start = jnp.arange(n_slc)[:, None] * SLC_BLOCK
    overlap = ((cmp_start < slc_start + SLC_BLOCK) & (cmp_start + CMP_BLOCK > slc_start)).astype(BF16)
    kv4 = 4 * g * d
    return _call_with_casts(
        functools.partial(_nsa_body, tq=tq, tk=tk, n_slc=n_slc, n_top=min(N_SLC, n_slc)),
        grid=(batch, g, nq),
        in_specs=[pl.BlockSpec((tq, r_ * d), lambda bi, gi, qi: (bi * nq + qi, col["q"] // r_ + gi)),
                  pl.BlockSpec((None, None, nc, d), lambda bi, gi, qi: (bi, gi, 0, 0)),
                  pl.BlockSpec((None, None, d, nc), lambda bi, gi, qi: (bi, gi, 0, 0)),
                  _resident((n_slc, nc), lambda bi, gi, qi: (0, 0)),
                  pl.BlockSpec((seq, kv4), lambda bi, gi, qi: (bi, col["kv4"])),
                  pl.BlockSpec((tq, LANES), lambda bi, gi, qi: (bi * nq + qi, col["gate"] + gi)),
                  pl.BlockSpec((1, d), lambda bi, gi, qi: (0, 0)),
                  pl.BlockSpec((N_BRANCH, d), lambda bi, gi, qi: (0, 0))],
        out_spec=pl.BlockSpec((tq, r_ * d), lambda bi, gi, qi: (bi * nq + qi, gi)),
        out_shape=jax.ShapeDtypeStruct((batch * seq, g * r_ * d), BF16),
        scratch_shapes=[pltpu.VMEM((seq, 2 * d), BF16), pltpu.VMEM((seq, d), BF16),
                        pltpu.VMEM((seq // tk, d + SUM_ROWS, tk), BF16),
                        pltpu.VMEM((seq // tq, d + SUM_ROWS, tq), BF16),
                        pltpu.VMEM((1, cols), F32), pltpu.VMEM((d + SUM_ROWS, cols), F32)],
        sem=("arbitrary", "arbitrary", "arbitrary"),
        name="nsa_attention",
        args=(p, kc, vct, overlap, p, p, q_gain.reshape(1, d), k_gain),
        casts=casts)


def _even_mixer(x, batch, seq, norm_g, w_in, q_gain, k_gain, cmp_pe, kw1, kw2, vw1, vw2,
                gmlp_norm, gmlp_ws, gmlp_b, casts, nsa_casts):
    d = HEAD_DIM
    dm = x.shape[1]
    gmlp_w = gmlp_ws.shape[0] * GMLP_GROUP_DIM
    qw = dm - gmlp_w
    g_kv = max(1, qw // d // NSA_GROUP)
    kv = g_kv * d
    n_kv = 2 * N_BRANCH
    n_gate = N_BRANCH * qw // d
    assert w_in.shape[1] == qw + n_kv * kv + n_gate + 2 * gmlp_w
    o_gl = qw + n_kv * kv
    o_u = o_gl + n_gate
    per_g = n_gate // g_kv
    zeros = lambda n: jnp.zeros((dm, n), BF16)
    cast = lambda a, b: w_in[:, a:b].astype(BF16)
    pieces = [cast(0, qw), cast(qw + 2 * kv, o_gl), cast(qw, qw + 2 * kv)]
    for gi in range(g_kv):
        pieces += [cast(o_gl + gi * per_g, o_gl + (gi + 1) * per_g), zeros(LANES - per_g)]
    nsa_w = o_gl + g_kv * LANES
    u_blk = -(-nsa_w // gmlp_w)
    pieces += [zeros(u_blk * gmlp_w - nsa_w), cast(o_u, o_u + 2 * gmlp_w)]
    w_cat = jnp.concatenate(pieces, axis=1)
    tn = w_cat.shape[1]
    assert qw % (4 * kv) == 0
    col = {"q": 0, "kv4": qw // (4 * kv), "kc": (qw + 4 * kv) // d, "vc": (qw + 5 * kv) // d, "gate": o_gl // d}

    p, cast_w = norm_matmul(x, norm_g, w_cat, tm=512, tn=tn, rc=256, casts=casts)

    hd = CMP_STRIDE * d
    kc, vct = compress(p, cmp_pe, kw1.reshape(2, hd, d).astype(BF16), kw2.astype(BF16),
                       vw1.reshape(2, hd, d).astype(BF16), vw2.astype(BF16), k_gain[0:1],
                       batch=batch, seq=seq, g_kv=g_kv, k_blk=col["kc"], v_blk=col["vc"])
    oa, nsa_cast_w = nsa_attention(p, kc, vct, q_gain, k_gain, batch=batch, seq=seq, col=col, tq=512, tk=256,
                                   casts=nsa_casts)
    ob = gmlp(p, gmlp_norm, gmlp_ws, gmlp_b, u_blk=u_blk, v_blk=u_blk + 1, tt=512)
    return oa, ob, cast_w, nsa_cast_w


def kernel(x, ev_norm, ev_w_in, ev_q_gain, ev_k_gain, ev_cmp_pe, ev_cmp_k_w1, ev_cmp_k_w2, ev_cmp_v_w1, ev_cmp_v_w2, ev_gmlp_norm, ev_gmlp_ws, ev_gmlp_b, ev_w_out, od_norm, od_w_in, od_conv_w, od_w_out, ffn_norm, ffn_w_in, ffn_conv_w, ffn_conv_b, ffn_w_down):
    batch, seq, dm = x.shape
    depth = ffn_norm.shape[0]
    no_bias = jnp.zeros((od_conv_w.shape[0], od_conv_w.shape[2]), F32)
    in_tiles = dict(seq=seq, tm=1024, tn=512, rc=256)
    sconv_tiles = dict(seq=seq, tm=512, tn=od_w_in.shape[2] // 3, rc=256)
    h = x.reshape(batch * seq, dm)
    ffn_in_w = od_in_w = None
    for i in range(depth):
        j = i // 2
        if i % 2 == 0:
            oa, ob, (out_w,), cast_w = _even_mixer(
                h, batch, seq, ev_norm[j], ev_w_in[j], ev_q_gain[j], ev_k_gain[j], ev_cmp_pe[j],
                ev_cmp_k_w1[j], ev_cmp_k_w2[j], ev_cmp_v_w1[j], ev_cmp_v_w2[j], ev_gmlp_norm[j],
                ev_gmlp_ws[j], ev_gmlp_b[j], [(ev_w_out, j)], [(ffn_w_in, i)] if ffn_in_w is None else [])
            ffn_in_w = cast_w[0] if cast_w else ffn_in_w
            h, _ = matmul_residual(h, [oa, ob], out_w, tm=512)
        else:
            m, (out_w,) = gated_in_proj(h, od_norm, od_in_w, 0, od_conv_w, no_bias, layer=j, mode="sconv",
                                        casts=[(od_w_out, j)], **sconv_tiles)
            h, _ = matmul_residual(h, [m], out_w, tm=512)
        next_odd = i + 1 < depth and (i + 1) % 2 == 1
        casts = [(ffn_w_down, i)] + ([(od_w_in, (i + 1) // 2)] if next_odd else [])
        f, cast_w = gated_in_proj(h, ffn_norm, ffn_in_w, 0, ffn_conv_w, ffn_conv_b, layer=i, mode="ffn",
                                  casts=casts, **in_tiles)
        od_in_w = cast_w[1] if next_odd else None
        h, cast_next = matmul_residual(h, [f], cast_w[0], tm=256,
                                       casts=[(ffn_w_in, i + 1)] if i + 1 < depth else [])
        ffn_in_w = cast_next[0] if cast_next else None
    return h.reshape(batch, seq, dm)
```

```python
import functools

import jax
import jax.numpy as jnp
from jax import lax
from jax.experimental import pallas as pl
from jax.experimental.pallas import tpu as pltpu

F32 = jnp.float32
BF16 = jnp.bfloat16

HEAD_DIM = 128
NSA_GROUP = 4
N_BRANCH = 3
CMP_BLOCK = 32
CMP_STRIDE = 16
SLC_BLOCK = 64
SLC_SHIFT = SLC_BLOCK.bit_length() - 1
LOG2_E = 1.4426950408889634
N_SLC = 16
N_LOCAL_SLC = 2
WINDOW = 512
FORCE_SCORE = 1e9
GMLP_GROUP_DIM = 128
GMLP_CHUNK = 128
CONV_WIDTH = 3
EPS = 1e-6
NEG = -1e30

LANES = 128
F32_SUBLANES = 8
BF16_SUBLANES = 16
SUM_ROWS = BF16_SUBLANES
VMEM_LIMIT = 56 * 1024 * 1024


def _params(*sem):
    return pltpu.CompilerParams(dimension_semantics=sem, vmem_limit_bytes=VMEM_LIMIT)


def _dot(a, b):
    return jnp.dot(a, b, preferred_element_type=F32)


def _rms(x, gain):
    return x * lax.rsqrt(jnp.mean(x * x, axis=-1, keepdims=True) + EPS) * gain


def _iota(shape, axis):
    return lax.broadcasted_iota(jnp.int32, shape, axis)


def _resident(shape, index_map):
    return pl.BlockSpec(shape, index_map, pipeline_mode=pl.Buffered(1))


def _cast_row_blocks(rows, n_steps):
    for nb in range(min(n_steps, rows // BF16_SUBLANES), 0, -1):
        if rows % nb == 0 and (rows // nb) % BF16_SUBLANES == 0:
            return nb
    raise ValueError(f"cannot split {rows} rows")


def _call_with_casts(body, *, grid, in_specs, out_spec, out_shape, scratch_shapes, sem, name, args, casts):
    n_steps = 1
    for n in grid:
        n_steps *= n

    def step(*idx):
        s = idx[0]
        for k, n in zip(idx[1:], grid[1:]):
            s = s * n + k
        return s

    n_in, n_cast = len(in_specs), len(casts)
    src_specs, dst_specs, dst_shapes = [], [], []
    for w, layer in casts:
        _, rows, c = w.shape
        nb = _cast_row_blocks(rows, n_steps)
        blk = functools.partial(lambda *idx, nb: jnp.minimum(step(*idx), nb - 1), nb=nb)
        src_specs.append(pl.BlockSpec((None, rows // nb, c),
                                      functools.partial(lambda *idx, blk, layer: (layer, blk(*idx), 0),
                                                        blk=blk, layer=layer)))
        dst_specs.append(pl.BlockSpec((None, rows // nb, c),
                                      functools.partial(lambda *idx, blk: (0, blk(*idx), 0), blk=blk)))
        dst_shapes.append(jax.ShapeDtypeStruct((1, rows, c), BF16))

    def with_casts(*refs):
        ins, src, rest = refs[:n_in], refs[n_in:n_in + n_cast], refs[n_in + n_cast:]
        out, dst, scratch = rest[0], rest[1:1 + n_cast], rest[1 + n_cast:]
        for s_ref, d_ref in zip(src, dst):
            d_ref[...] = s_ref[...].astype(BF16)
        body(*ins, out, *scratch)

    outs = pl.pallas_call(
        with_casts,
        grid=grid,
        in_specs=list(in_specs) + src_specs,
        out_specs=[out_spec] + dst_specs,
        out_shape=[out_shape] + dst_shapes,
        scratch_shapes=scratch_shapes,
        compiler_params=_params(*sem),
        name=name,
    )(*args, *[w for w, _ in casts])
    return outs[0], list(outs[1:])


def _normed_rows(x_ref, g_ref, xn_ref, rows, first):
    if not first:
        return xn_ref[rows, :]
    xc = _rms(x_ref[rows, :], g_ref[...]).astype(BF16)
    xn_ref[rows, :] = xc
    return xc


def _first_then_rest(j, body):
    pl.when(j == 0)(functools.partial(body, True))
    pl.when(j != 0)(functools.partial(body, False))


def _norm_matmul_body(x_ref, g_ref, w_ref, o_ref, xn_ref, *, rc):
    def body(first):
        for c in range(x_ref.shape[0] // rc):
            rows = slice(c * rc, (c + 1) * rc)
            xc = _normed_rows(x_ref, g_ref, xn_ref, rows, first)
            o_ref[rows, :] = _dot(xc, w_ref[...]).astype(o_ref.dtype)

    _first_then_rest(pl.program_id(1), body)


def norm_matmul(x, g, w, *, tm, tn, rc, casts=()):
    t, d = x.shape
    n = w.shape[1]
    assert t % tm == 0 and n % tn == 0 and tm % rc == 0
    return _call_with_casts(
        functools.partial(_norm_matmul_body, rc=rc),
        grid=(t // tm, n // tn),
        in_specs=[pl.BlockSpec((tm, d), lambda i, j: (i, 0)),
                  _resident((1, d), lambda i, j: (0, 0)),
                  (_resident if n == tn else pl.BlockSpec)((d, tn), lambda i, j: (0, j))],
        out_spec=pl.BlockSpec((tm, tn), lambda i, j: (i, j)),
        out_shape=jax.ShapeDtypeStruct((t, n), BF16),
        scratch_shapes=[pltpu.VMEM((tm, d), BF16)],
        sem=("arbitrary", "arbitrary"),
        name="norm_matmul",
        args=(x, g.reshape(1, d), w),
        casts=casts)


def _shift_rows(cur, tail, k):
    out = pltpu.roll(cur, k, axis=0)
    row = _iota(cur.shape, 0)
    for r in range(k):
        src = F32_SUBLANES - k + r
        out = jnp.where(row == r, tail[src:src + 1, :], out)
    return out


def _conv3(cur, tail, cw):
    return (cw[0:1, :] * _shift_rows(cur, tail, 2) + cw[1:2, :] * _shift_rows(cur, tail, 1)
            + cw[2:3, :] * cur)


def _gated_in_body(x_ref, gn_ref, *rest, mode, rc, seq_tiles):
    n_parts = 2 if mode == "ffn" else 3
    w_refs = rest[:n_parts]
    cw_ref, cb_ref, h_ref, xn_ref, tail_ref = rest[n_parts:]
    i, j = pl.program_id(0), pl.program_id(1)
    tm = x_ref.shape[0]

    @pl.when(i % seq_tiles == 0)
    def _():
        tail_ref[j] = jnp.zeros(tail_ref.shape[1:], F32)

    def body(first):
        cw = cw_ref[...]
        tail = tail_ref[j]
        for c in range(tm // rc):
            rows = slice(c * rc, (c + 1) * rc)
            xc = _normed_rows(x_ref, gn_ref, xn_ref, rows, first)
            parts = [_dot(xc, w_ref[...]) for w_ref in w_refs]
            if mode == "ffn":
                g, u = parts
                h = jax.nn.silu(_conv3(g, tail, cw) + cb_ref[...]) * u
                tail = g[rc - F32_SUBLANES:, :]
            else:
                a, cc, dd = parts
                m = cc * dd
                h = a * _conv3(m, tail, cw)
                tail = m[rc - F32_SUBLANES:, :]
            h_ref[rows, :] = h.astype(h_ref.dtype)
        tail_ref[j] = tail

    _first_then_rest(j, body)


def gated_in_proj(x, g, w, w_layer, conv_w, conv_b, *, layer, mode, seq, tm, tn, rc, casts=()):
    t, d = x.shape
    n_parts = 2 if mode == "ffn" else 3
    f = w.shape[2] // n_parts
    assert t % tm == 0 and f % tn == 0 and seq % tm == 0 and tm % rc == 0
    nj = f // tn
    w_spec = _resident if nj == 1 else pl.BlockSpec
    w_specs = [w_spec((None, d, tn), functools.partial(lambda i, j, p: (w_layer, 0, p * nj + j), p=p))
               for p in range(n_parts)]
    return _call_with_casts(
        functools.partial(_gated_in_body, mode=mode, rc=rc, seq_tiles=seq // tm),
        grid=(t // tm, nj),
        in_specs=[pl.BlockSpec((tm, d), lambda i, j: (i, 0)),
                  _resident((None, 1, d), lambda i, j: (layer, 0, 0))] + w_specs + [
                  pl.BlockSpec((None, CONV_WIDTH, tn), lambda i, j: (layer, 0, j)),
                  pl.BlockSpec((None, 1, tn), lambda i, j: (layer, 0, j))],
        out_spec=pl.BlockSpec((tm, tn), lambda i, j: (i, j)),
        out_shape=jax.ShapeDtypeStruct((t, f), BF16),
        scratch_shapes=[pltpu.VMEM((tm, d), BF16), pltpu.VMEM((nj, F32_SUBLANES, tn), F32)],
        sem=("arbitrary", "arbitrary"),
        name=mode + "_in_proj",
        args=(x, g[:, None, :], *([w] * n_parts), conv_w, conv_b[:, None, :]),
        casts=casts)


def _matmul_residual_body(x_ref, *rest):
    n = (len(rest) - 1) // 2
    h_refs, w_refs, o_ref = rest[:n], rest[n:2 * n], rest[2 * n]
    acc = x_ref[...]
    for h_ref, w_ref in zip(h_refs, w_refs):
        acc = acc + _dot(h_ref[...], w_ref[...])
    o_ref[...] = acc


def matmul_residual(x, hs, w, *, tm, casts=()):
    t, d = x.shape
    k = hs[0].shape[1]
    assert all(h.shape == (t, k) for h in hs) and w.shape == (1, k * len(hs), d) and t % tm == 0
    h_specs = [pl.BlockSpec((tm, k), lambda i: (i, 0)) for _ in hs]
    w_specs = [_resident((None, k, d), functools.partial(lambda i, p: (0, p, 0), p=p))
               for p in range(len(hs))]
    return _call_with_casts(
        _matmul_residual_body,
        grid=(t // tm,),
        in_specs=[pl.BlockSpec((tm, d), lambda i: (i, 0))] + h_specs + w_specs,
        out_spec=pl.BlockSpec((tm, d), lambda i: (i, 0)),
        out_shape=jax.ShapeDtypeStruct((t, d), F32),
        scratch_shapes=[],
        sem=("arbitrary",),
        name="out_proj",
        args=(x, *hs, *([w] * len(hs))),
        casts=casts)


def _gmlp_body(u_ref, v_ref, gain_ref, ws_ref, bs_ref, o_ref, *, chunks):
    c, gd = GMLP_CHUNK, GMLP_GROUP_DIM
    causal = _iota((c, c), 1) <= _iota((c, c), 0)
    for g in range(ws_ref.shape[0]):
        cols = slice(g * gd, (g + 1) * gd)
        v = jax.nn.gelu(v_ref[:, cols].astype(F32))
        vc = v - jnp.mean(v, axis=-1, keepdims=True)
        vn = vc * lax.rsqrt(jnp.mean(vc * vc, axis=-1, keepdims=True) + EPS) * gain_ref[g]
        vn = vn.astype(BF16)
        w = jnp.where(causal, ws_ref[g], 0.0).astype(BF16)
        bias = bs_ref[g]
        for ci in range(chunks):
            rows = slice(ci * c, (ci + 1) * c)
            mixed = _dot(w, vn[rows, :]) + bias
            u = jax.nn.gelu(u_ref[rows, cols].astype(F32))
            o_ref[rows, cols] = (u * mixed).astype(o_ref.dtype)


def gmlp(p, gain, ws, bs, *, u_blk, v_blk, tt):
    t = p.shape[0]
    groups, c, _ = ws.shape
    gd = GMLP_GROUP_DIM
    width = groups * gd
    assert t % tt == 0 and tt % c == 0
    full = lambda shape: _resident(shape, lambda i: (0,) * len(shape))
    return pl.pallas_call(
        functools.partial(_gmlp_body, chunks=tt // c),
        grid=(t // tt,),
        in_specs=[pl.BlockSpec((tt, width), lambda i: (i, u_blk)),
                  pl.BlockSpec((tt, width), lambda i: (i, v_blk)),
                  full((groups, 1, gd)), full((groups, c, c)), full((groups, c, 1))],
        out_specs=pl.BlockSpec((tt, width), lambda i: (i, 0)),
        out_shape=jax.ShapeDtypeStruct((t, width), BF16),
        compiler_params=_params("parallel"),
        name="gmlp",
    )(p, p, gain.reshape(groups, 1, gd), ws, bs.reshape(groups, c, 1))


def _compress_body(k_ref, v_ref, pe_ref, kw1_ref, kw2_ref, vw1_ref, vw2_ref, kg_ref, kc_ref, vct_ref, x_ref):
    st = CMP_STRIDE
    n = x_ref.shape[0] // st

    def mlp(src_ref, w1_ref, w2_ref):
        x_ref[...] = src_ref[...].astype(F32)

        def half(part):
            return jnp.concatenate(
                [(x_ref[pl.ds(l, n, stride=st), :] + pe_ref[part * st + l:part * st + l + 1, :]).astype(BF16)
                 for l in range(st)], axis=1)

        a = _dot(half(0), w1_ref[0])
        b = _dot(half(1), w1_ref[1])
        hid = jax.nn.gelu(a + pltpu.roll(b, n - 1, axis=0))
        return _dot(hid.astype(BF16), w2_ref[...])

    kc_ref[...] = _rms(mlp(k_ref, kw1_ref, kw2_ref), kg_ref[...]).astype(kc_ref.dtype)
    vct_ref[...] = mlp(v_ref, vw1_ref, vw2_ref).T.astype(vct_ref.dtype)


def compress(p, pe, kw1, kw2, vw1, vw2, kgain, *, batch, seq, g_kv, k_blk, v_blk):
    d = HEAD_DIM
    st = CMP_STRIDE
    n = seq // st
    hd = st * d
    assert CMP_BLOCK == 2 * st and pe.shape == (CMP_BLOCK, d) and kw1.shape == (2, hd, d)
    full = lambda shape: pl.BlockSpec(shape, lambda i, j: (0,) * len(shape))
    return pl.pallas_call(
        _compress_body,
        grid=(batch, g_kv),
        in_specs=[pl.BlockSpec((seq, d), lambda i, j: (i, k_blk + j)),
                  pl.BlockSpec((seq, d), lambda i, j: (i, v_blk + j)),
                  full((CMP_BLOCK, d)), full((2, hd, d)), full((d, d)),
                  full((2, hd, d)), full((d, d)), full((1, d))],
        out_specs=[pl.BlockSpec((None, None, n, d), lambda i, j: (i, j, 0, 0)),
                   pl.BlockSpec((None, None, d, n), lambda i, j: (i, j, 0, 0))],
        out_shape=[jax.ShapeDtypeStruct((batch, g_kv, n, d), BF16),
                   jax.ShapeDtypeStruct((batch, g_kv, d, n), BF16)],
        scratch_shapes=[pltpu.VMEM((seq, d), F32)],
        compiler_params=_params("parallel", "arbitrary"),
        name="nsa_compress",
    )(p, p, pe, kw1, kw2, vw1, vw2, kgain)


def _col_softmax(s, mask, *, may_be_empty, with_sum=True):
    s = jnp.where(mask, s, NEG)
    p = jnp.exp2(s - jnp.max(s, axis=0, keepdims=True))
    if may_be_empty:
        p = jnp.where(mask, p, 0.0)
    return (p, jnp.sum(p, axis=0, keepdims=True)) if with_sum else p


def _nsa_body(q_ref, kc_ref, vct_ref, ov_ref, kv_ref, gl_ref, qg_ref, kg_ref, o_ref,
              ksx_ref, kwn_ref, vst_ref, vwt_ref, m_ref, acc_ref, *, tq, tk, n_slc, n_top):
    r_, d = NSA_GROUP, HEAD_DIM
    cols = r_ * tq
    nc = kc_ref.shape[0]
    seq = kv_ref.shape[0]
    g_kv = kv_ref.shape[1] // (4 * d)
    qi = pl.program_id(2)
    q0 = qi * tq
    head = lambda r: slice(r * tq, (r + 1) * tq)

    def stage_kv(g):
        sect = lambda n: slice((n * g_kv + g) * d, (n * g_kv + g + 1) * d)
        ksx_ref[:, 0:d] = _rms(kv_ref[:, sect(0)].astype(F32), kg_ref[1:2, :]).astype(BF16)
        ksx_ref[:, d:2 * d] = (_iota((seq, d), 0) >> SLC_SHIFT == _iota((seq, d), 1)).astype(BF16)
        kwn_ref[...] = _rms(kv_ref[:, sect(2)].astype(F32), kg_ref[2:3, :]).astype(BF16)
        for c in range(seq // tk):
            vst_ref[c, 0:d, :] = kv_ref[c * tk:(c + 1) * tk, sect(1)].astype(F32).T.astype(BF16)
            vst_ref[c, d:, :] = jnp.ones((SUM_ROWS, tk), BF16)
        for c in range(seq // tq):
            vwt_ref[c, 0:d, :] = kv_ref[c * tq:(c + 1) * tq, sect(3)].astype(F32).T.astype(BF16)
            vwt_ref[c, d:, :] = jnp.ones((SUM_ROWS, tq), BF16)

    for g in range(g_kv):
        pl.when((qi == 0) & (pl.program_id(1) == g))(functools.partial(stage_kv, g))

    qraw = q_ref[...].astype(F32)
    qt = jnp.concatenate(
        [(_rms(qraw[:, r * d:(r + 1) * d], qg_ref[...]) * (d ** -0.5 * LOG2_E)).T for r in range(r_)],
        axis=1).astype(BF16)

    sc = _dot(kc_ref[...], qt)
    mask_c = _iota((nc, tq), 0) * CMP_STRIDE + (CMP_BLOCK - 1) <= q0 + _iota((nc, tq), 1)
    o_cmp = []
    psum = None
    for r in range(r_):
        p, den = _col_softmax(sc[:, head(r)], mask_c, may_be_empty=True)
        p = p / jnp.maximum(den, 1e-30)
        psum = p if psum is None else psum + p
        o_cmp.append(_dot(vct_ref[...], p.astype(BF16)))

    overlap = ov_ref[...]
    hi = psum.astype(BF16)
    r1 = psum - hi.astype(F32)
    lo = r1.astype(BF16)
    lo2 = (r1 - lo.astype(F32)).astype(BF16)
    imp = _dot(overlap, hi) + _dot(overlap, lo) + _dot(overlap, lo2)

    j_s = _iota((n_slc, tq), 0)
    pos_s = q0 + _iota((n_slc, tq), 1)
    dlt = (pos_s >> SLC_SHIFT) - j_s
    forced = (j_s == 0) | ((dlt >= 0) & (dlt < N_LOCAL_SLC))
    val = jnp.where(dlt >= 0, imp, NEG)
    val = jnp.where(forced, FORCE_SCORE, val)
    sub = F32_SUBLANES
    groups = [val[g0:g0 + sub, :] for g0 in range(0, n_slc, sub)]
    cnts = [jnp.zeros((sub, tq), jnp.int32) for _ in groups]
    j_g = _iota((sub, tq), 0)
    for i in range(n_slc):
        vi = val[i:i + 1, :]
        for gi, vg in enumerate(groups):
            g0 = gi * sub
            if g0 > i:
                beats = vi >= vg
            elif g0 + sub - 1 < i:
                beats = vi > vg
            else:
                beats = (vi > vg) | ((vi == vg) & (j_g + g0 > i))
            cnts[gi] = cnts[gi] + beats.astype(jnp.int32)
    cnt = jnp.concatenate(cnts, axis=0)
    bias = jnp.where(cnt < n_top, 0.0, NEG)
    if n_slc < d:
        bias = jnp.concatenate([bias, jnp.zeros((d - n_slc, tq), F32)], axis=0)
    qx = jnp.concatenate([qt, jnp.concatenate([bias.astype(BF16)] * r_, axis=1)], axis=0)

    m_ref[...] = jnp.full((1, cols), NEG, F32)
    acc_ref[...] = jnp.zeros((d + SUM_ROWS, cols), F32)

    def scores(kt):
        k0 = pl.multiple_of(kt * tk, tk)
        return _dot(ksx_ref[pl.ds(k0, tk), :], qx)

    def fold(kt, r, s, causal):
        if causal is not None:
            s = jnp.where(causal >= kt * tk, s, NEG)
        m_old = m_ref[:, head(r)]
        m_new = jnp.maximum(m_old, jnp.max(s, axis=0, keepdims=True))
        alpha = jnp.exp2(m_old - m_new)
        p = jnp.exp2(s - m_new)
        acc_ref[:, head(r)] = alpha * acc_ref[:, head(r)] + _dot(vst_ref[kt], p.astype(BF16))
        m_ref[:, head(r)] = m_new

    def slc_tiles(kts, causal):
        s_all = [scores(kt) for kt in kts]
        for kt, s in zip(kts, s_all):
            for r in range(r_):
                fold(kt, r, s[:, head(r)], causal)

    n_quads = q0 // (4 * tk)

    def quad(i, carry):
        slc_tiles([4 * i + dd for dd in range(4)], None)
        return carry

    lax.fori_loop(0, n_quads, quad, 0)
    causal_gap = q0 + _iota((tk, tq), 1) - _iota((tk, tq), 0)
    n_tiles = (q0 + tq + tk - 1) // tk

    def causal_pair(i, carry):
        slc_tiles([4 * n_quads + 2 * i, 4 * n_quads + 2 * i + 1], causal_gap)
        return carry

    lax.fori_loop(0, (n_tiles - 4 * n_quads + 1) // 2, causal_pair, 0)

    nw = WINDOW // tq + 1
    wc = jnp.maximum(qi - WINDOW // tq, 0)
    w0 = pl.multiple_of(wc * tq, tq)
    sw = _dot(kwn_ref[pl.ds(w0, nw * tq), :], qt)
    gap_w = q0 + _iota((nw * tq, tq), 1) - (w0 + _iota((nw * tq, tq), 0))
    mask_w = (gap_w >= 0) & (gap_w < WINDOW)
    o_win = []
    for r in range(r_):
        p = _col_softmax(sw[:, head(r)], mask_w, may_be_empty=False, with_sum=False).astype(BF16)
        o = _dot(vwt_ref[wc], p[0:tq, :])
        for c in range(1, nw):
            o = o + _dot(vwt_ref[wc + c], p[c * tq:(c + 1) * tq, :])
        o_win.append(o[0:d, :] / o[d:d + 1, :])

    gate = jax.nn.sigmoid(gl_ref[...].astype(F32)).T
    for r in range(r_):
        c = N_BRANCH * r
        o_slc = acc_ref[0:d, head(r)] / acc_ref[d:d + 1, head(r)]
        o = (gate[c:c + 1, :] * o_cmp[r] + gate[c + 1:c + 2, :] * o_slc
             + gate[c + 2:c + 3, :] * o_win[r])
        o_ref[:, r * d:(r + 1) * d] = o.T.astype(o_ref.dtype)


def nsa_attention(p, kc, vct, q_gain, k_gain, *, batch, seq, col, tq, tk, casts=()):
    b, g, nc, d = kc.shape
    r_ = NSA_GROUP
    nq = seq // tq
    n_slc = seq // SLC_BLOCK
    assert seq % tq == 0 and seq % tk == 0 and WINDOW % tq == 0 and WINDOW + tq <= seq
    assert ((tq == tk and nq % 2 == 0) or tq == 2 * tk) and n_slc <= d
    cols = r_ * tq
    cmp_start = jnp.arange(nc)[None, :] * CMP_STRIDE
    slc_start = jnp.arange(n_slc)[:, None] * SLC_BLOCK
    overlap = ((cmp_start < slc_start + SLC_BLOCK) & (cmp_start + CMP_BLOCK > slc_start)).astype(BF16)
    kv4 = 4 * g * d
    return _call_with_casts(
        functools.partial(_nsa_body, tq=tq, tk=tk, n_slc=n_slc, n_top=min(N_SLC, n_slc)),
        grid=(batch, g, nq),
        in_specs=[pl.BlockSpec((tq, r_ * d), lambda bi, gi, qi: (bi * nq + qi, col["q"] // r_ + gi)),
                  pl.BlockSpec((None, None, nc, d), lambda bi, gi, qi: (bi, gi, 0, 0)),
                  pl.BlockSpec((None, None, d, nc), lambda bi, gi, qi: (bi, gi, 0, 0)),
                  _resident((n_slc, nc), lambda bi, gi, qi: (0, 0)),
                  pl.BlockSpec((seq, kv4), lambda bi, gi, qi: (bi, col["kv4"])),
                  pl.BlockSpec((tq, LANES), lambda bi, gi, qi: (bi * nq + qi, col["gate"] + gi)),
                  pl.BlockSpec((1, d), lambda bi, gi, qi: (0, 0)),
                  pl.BlockSpec((N_BRANCH, d), lambda bi, gi, qi: (0, 0))],
        out_spec=pl.BlockSpec((tq, r_ * d), lambda bi, gi, qi: (bi * nq + qi, gi)),
        out_shape=jax.ShapeDtypeStruct((batch * seq, g * r_ * d), BF16),
        scratch_shapes=[pltpu.VMEM((seq, 2 * d), BF16), pltpu.VMEM((seq, d), BF16),
                        pltpu.VMEM((seq // tk, d + SUM_ROWS, tk), BF16),
                        pltpu.VMEM((seq // tq, d + SUM_ROWS, tq), BF16),
                        pltpu.VMEM((1, cols), F32), pltpu.VMEM((d + SUM_ROWS, cols), F32)],
        sem=("arbitrary", "arbitrary", "arbitrary"),
        name="nsa_attention",
        args=(p, kc, vct, overlap, p, p, q_gain.reshape(1, d), k_gain),
        casts=casts)


def _even_mixer(x, batch, seq, norm_g, w_in, q_gain, k_gain, cmp_pe, kw1, kw2, vw1, vw2,
                gmlp_norm, gmlp_ws, gmlp_b, casts, nsa_casts):
    d = HEAD_DIM
    dm = x.shape[1]
    gmlp_w = gmlp_ws.shape[0] * GMLP_GROUP_DIM
    qw = dm - gmlp_w
    g_kv = max(1, qw // d // NSA_GROUP)
    kv = g_kv * d
    n_kv = 2 * N_BRANCH
    n_gate = N_BRANCH * qw // d
    assert w_in.shape[1] == qw + n_kv * kv + n_gate + 2 * gmlp_w
    o_gl = qw + n_kv * kv
    o_u = o_gl + n_gate
    per_g = n_gate // g_kv
    pad_cols = lambda w, n: jnp.pad(w, ((0, 0), (0, n - w.shape[1])))
    gate_cols = [pad_cols(w_in[:, o_gl + gi * per_g: o_gl + (gi + 1) * per_g], LANES) for gi in range(g_kv)]
    w_nsa = jnp.concatenate([w_in[:, :qw], w_in[:, qw + 2 * kv:o_gl], w_in[:, qw:qw + 2 * kv]] + gate_cols, axis=1)
    u_blk = -(-w_nsa.shape[1] // gmlp_w)
    w_cat = jnp.concatenate([pad_cols(w_nsa, u_blk * gmlp_w), w_in[:, o_u:]], axis=1).astype(BF16)
    tn = w_cat.shape[1]
    assert qw % (4 * kv) == 0
    col = {"q": 0, "kv4": qw // (4 * kv), "kc": (qw + 4 * kv) // d, "vc": (qw + 5 * kv) // d, "gate": o_gl // d}

    p, cast_w = norm_matmul(x, norm_g, w_cat, tm=512, tn=tn, rc=256, casts=casts)

    hd = CMP_STRIDE * d
    kc, vct = compress(p, cmp_pe, kw1.reshape(2, hd, d).astype(BF16), kw2.astype(BF16),
                       vw1.reshape(2, hd, d).astype(BF16), vw2.astype(BF16), k_gain[0:1],
                       batch=batch, seq=seq, g_kv=g_kv, k_blk=col["kc"], v_blk=col["vc"])
    oa, nsa_cast_w = nsa_attention(p, kc, vct, q_gain, k_gain, batch=batch, seq=seq, col=col, tq=512, tk=256,
                                   casts=nsa_casts)
    ob = gmlp(p, gmlp_norm, gmlp_ws, gmlp_b, u_blk=u_blk, v_blk=u_blk + 1, tt=512)
    return oa, ob, cast_w, nsa_cast_w


def kernel(x, ev_norm, ev_w_in, ev_q_gain, ev_k_gain, ev_cmp_pe, ev_cmp_k_w1, ev_cmp_k_w2, ev_cmp_v_w1, ev_cmp_v_w2, ev_gmlp_norm, ev_gmlp_ws, ev_gmlp_b, ev_w_out, od_norm, od_w_in, od_conv_w, od_w_out, ffn_norm, ffn_w_in, ffn_conv_w, ffn_conv_b, ffn_w_down):
    batch, seq, dm = x.shape
    depth = ffn_norm.shape[0]
    no_bias = jnp.zeros((od_conv_w.shape[0], od_conv_w.shape[2]), F32)
    in_tiles = dict(seq=seq, tm=1024, tn=512, rc=256)
    sconv_tiles = dict(seq=seq, tm=512, tn=od_w_in.shape[2] // 3, rc=256)
    h = x.reshape(batch * seq, dm)
    ffn_in_w = od_in_w = None
    for i in range(depth):
        j = i // 2
        if i % 2 == 0:
            oa, ob, (out_w,), cast_w = _even_mixer(
                h, batch, seq, ev_norm[j], ev_w_in[j], ev_q_gain[j], ev_k_gain[j], ev_cmp_pe[j],
                ev_cmp_k_w1[j], ev_cmp_k_w2[j], ev_cmp_v_w1[j], ev_cmp_v_w2[j], ev_gmlp_norm[j],
                ev_gmlp_ws[j], ev_gmlp_b[j], [(ev_w_out, j)], [(ffn_w_in, i)] if ffn_in_w is None else [])
            ffn_in_w = cast_w[0] if cast_w else ffn_in_w
            h, _ = matmul_residual(h, [oa, ob], out_w, tm=512)
        else:
            m, (out_w,) = gated_in_proj(h, od_norm, od_in_w, 0, od_conv_w, no_bias, layer=j, mode="sconv",
                                        casts=[(od_w_out, j)], **sconv_tiles)
            h, _ = matmul_residual(h, [m], out_w, tm=512)
        next_odd = i + 1 < depth and (i + 1) % 2 == 1
        casts = [(ffn_w_down, i)] + ([(od_w_in, (i + 1) // 2)] if next_odd else [])
        f, cast_w = gated_in_proj(h, ffn_norm, ffn_in_w, 0, ffn_conv_w, ffn_conv_b, layer=i, mode="ffn",
                                  casts=casts, **in_tiles)
        od_in_w = cast_w[1] if next_odd else None
        h, cast_next = matmul_residual(h, [f], cast_w[0], tm=256,
                                       casts=[(ffn_w_in, i + 1)] if i + 1 < depth else [])
        ffn_in_w = cast_next[0] if cast_next else None
    return h.reshape(batch, seq, dm)
```

```python
import functools

import jax
import jax.numpy as jnp
from jax import lax
from jax.experimental import pallas as pl
from jax.experimental.pallas import tpu as pltpu

F32 = jnp.float32
BF16 = jnp.bfloat16

HEAD_DIM = 128
NSA_GROUP = 4
N_BRANCH = 3
CMP_BLOCK = 32
CMP_STRIDE = 16
SLC_BLOCK = 64
SLC_SHIFT = SLC_BLOCK.bit_length() - 1
LOG2_E = 1.4426950408889634
N_SLC = 16
N_LOCAL_SLC = 2
WINDOW = 512
FORCE_SCORE = 1e9
GMLP_GROUP_DIM = 128
GMLP_CHUNK = 128
CONV_WIDTH = 3
EPS = 1e-6
NEG = -1e30

LANES = 128
F32_SUBLANES = 8
BF16_SUBLANES = 16
SUM_ROWS = BF16_SUBLANES
VMEM_LIMIT = 56 * 1024 * 1024


def _params(*sem):
    return pltpu.CompilerParams(dimension_semantics=sem, vmem_limit_bytes=VMEM_LIMIT)


def _dot(a, b):
    return jnp.dot(a, b, preferred_element_type=F32)


def _rms(x, gain):
    return x * lax.rsqrt(jnp.mean(x * x, axis=-1, keepdims=True) + EPS) * gain


def _iota(shape, axis):
    return lax.broadcasted_iota(jnp.int32, shape, axis)


def _resident(shape, index_map):
    return pl.BlockSpec(shape, index_map, pipeline_mode=pl.Buffered(1))


def _cast_row_blocks(rows, n_steps):
    for nb in range(min(n_steps, rows // BF16_SUBLANES), 0, -1):
        if rows % nb == 0 and (rows // nb) % BF16_SUBLANES == 0:
            return nb
    raise ValueError(f"cannot split {rows} rows")


def _call_with_casts(body, *, grid, in_specs, out_spec, out_shape, scratch_shapes, sem, name, args, casts):
    n_steps = 1
    for n in grid:
        n_steps *= n

    def step(*idx):
        s = idx[0]
        for k, n in zip(idx[1:], grid[1:]):
            s = s * n + k
        return s

    n_in, n_cast = len(in_specs), len(casts)
    src_specs, dst_specs, dst_shapes = [], [], []
    for w, layer in casts:
        _, rows, c = w.shape
        nb = _cast_row_blocks(rows, n_steps)
        blk = functools.partial(lambda *idx, nb: jnp.minimum(step(*idx), nb - 1), nb=nb)
        src_specs.append(pl.BlockSpec((None, rows // nb, c),
                                      functools.partial(lambda *idx, blk, layer: (layer, blk(*idx), 0),
                                                        blk=blk, layer=layer)))
        dst_specs.append(pl.BlockSpec((None, rows // nb, c),
                                      functools.partial(lambda *idx, blk: (0, blk(*idx), 0), blk=blk)))
        dst_shapes.append(jax.ShapeDtypeStruct((1, rows, c), BF16))

    def with_casts(*refs):
        ins, src, rest = refs[:n_in], refs[n_in:n_in + n_cast], refs[n_in + n_cast:]
        out, dst, scratch = rest[0], rest[1:1 + n_cast], rest[1 + n_cast:]
        for s_ref, d_ref in zip(src, dst):
            d_ref[...] = s_ref[...].astype(BF16)
        body(*ins, out, *scratch)

    outs = pl.pallas_call(
        with_casts,
        grid=grid,
        in_specs=list(in_specs) + src_specs,
        out_specs=[out_spec] + dst_specs,
        out_shape=[out_shape] + dst_shapes,
        scratch_shapes=scratch_shapes,
        compiler_params=_params(*sem),
        name=name,
    )(*args, *[w for w, _ in casts])
    return outs[0], list(outs[1:])


def _normed_rows(x_ref, g_ref, xn_ref, rows, first):
    if not first:
        return xn_ref[rows, :]
    xc = _rms(x_ref[rows, :], g_ref[...]).astype(BF16)
    xn_ref[rows, :] = xc
    return xc


def _first_then_rest(j, body):
    pl.when(j == 0)(functools.partial(body, True))
    pl.when(j != 0)(functools.partial(body, False))


def _norm_matmul_body(x_ref, g_ref, w_ref, o_ref, xn_ref, *, rc):
    def body(first):
        for c in range(x_ref.shape[0] // rc):
            rows = slice(c * rc, (c + 1) * rc)
            xc = _normed_rows(x_ref, g_ref, xn_ref, rows, first)
            o_ref[rows, :] = _dot(xc, w_ref[...]).astype(o_ref.dtype)

    _first_then_rest(pl.program_id(1), body)


def norm_matmul(x, g, w, *, tm, tn, rc, casts=()):
    t, d = x.shape
    n = w.shape[1]
    assert t % tm == 0 and n % tn == 0 and tm % rc == 0
    return _call_with_casts(
        functools.partial(_norm_matmul_body, rc=rc),
        grid=(t // tm, n // tn),
        in_specs=[pl.BlockSpec((tm, d), lambda i, j: (i, 0)),
                  _resident((1, d), lambda i, j: (0, 0)),
                  (_resident if n == tn else pl.BlockSpec)((d, tn), lambda i, j: (0, j))],
        out_spec=pl.BlockSpec((tm, tn), lambda i, j: (i, j)),
        out_shape=jax.ShapeDtypeStruct((t, n), BF16),
        scratch_shapes=[pltpu.VMEM((tm, d), BF16)],
        sem=("arbitrary", "arbitrary"),
        name="norm_matmul",
        args=(x, g.reshape(1, d), w),
        casts=casts)


def _shift_rows(cur, tail, k):
    out = pltpu.roll(cur, k, axis=0)
    row = _iota(cur.shape, 0)
    for r in range(k):
        src = F32_SUBLANES - k + r
        out = jnp.where(row == r, tail[src:src + 1, :], out)
    return out


def _conv3(cur, tail, cw):
    return (cw[0:1, :] * _shift_rows(cur, tail, 2) + cw[1:2, :] * _shift_rows(cur, tail, 1)
            + cw[2:3, :] * cur)


def _gated_in_body(x_ref, gn_ref, *rest, mode, rc, seq_tiles):
    n_parts = 2 if mode == "ffn" else 3
    w_refs = rest[:n_parts]
    cw_ref, cb_ref, h_ref, xn_ref, tail_ref = rest[n_parts:]
    i, j = pl.program_id(0), pl.program_id(1)
    tm = x_ref.shape[0]

    @pl.when(i % seq_tiles == 0)
    def _():
        tail_ref[j] = jnp.zeros(tail_ref.shape[1:], F32)

    def body(first):
        cw = cw_ref[...]
        tail = tail_ref[j]
        for c in range(tm // rc):
            rows = slice(c * rc, (c + 1) * rc)
            xc = _normed_rows(x_ref, gn_ref, xn_ref, rows, first)
            parts = [_dot(xc, w_ref[...]) for w_ref in w_refs]
            if mode == "ffn":
                g, u = parts
                h = jax.nn.silu(_conv3(g, tail, cw) + cb_ref[...]) * u
                tail = g[rc - F32_SUBLANES:, :]
            else:
                a, cc, dd = parts
                m = cc * dd
                h = a * _conv3(m, tail, cw)
                tail = m[rc - F32_SUBLANES:, :]
            h_ref[rows, :] = h.astype(h_ref.dtype)
        tail_ref[j] = tail

    _first_then_rest(j, body)


def gated_in_proj(x, g, w, w_layer, conv_w, conv_b, *, layer, mode, seq, tm, tn, rc, casts=()):
    t, d = x.shape
    n_parts = 2 if mode == "ffn" else 3
    f = w.shape[2] // n_parts
    assert t % tm == 0 and f % tn == 0 and seq % tm == 0 and tm % rc == 0
    nj = f // tn
    w_spec = _resident if nj == 1 else pl.BlockSpec
    w_specs = [w_spec((None, d, tn), functools.partial(lambda i, j, p: (w_layer, 0, p * nj + j), p=p))
               for p in range(n_parts)]
    return _call_with_casts(
        functools.partial(_gated_in_body, mode=mode, rc=rc, seq_tiles=seq // tm),
        grid=(t // tm, nj),
        in_specs=[pl.BlockSpec((tm, d), lambda i, j: (i, 0)),
                  _resident((None, 1, d), lambda i, j: (layer, 0, 0))] + w_specs + [
                  pl.BlockSpec((None, CONV_WIDTH, tn), lambda i, j: (layer, 0, j)),
                  pl.BlockSpec((None, 1, tn), lambda i, j: (layer, 0, j))],
        out_spec=pl.BlockSpec((tm, tn), lambda i, j: (i, j)),
        out_shape=jax.ShapeDtypeStruct((t, f), BF16),
        scratch_shapes=[pltpu.VMEM((tm, d), BF16), pltpu.VMEM((nj, F32_SUBLANES, tn), F32)],
        sem=("arbitrary", "arbitrary"),
        name=mode + "_in_proj",
        args=(x, g[:, None, :], *([w] * n_parts), conv_w, conv_b[:, None, :]),
        casts=casts)


def _matmul_residual_body(x_ref, *rest):
    n = (len(rest) - 1) // 2
    h_refs, w_refs, o_ref = rest[:n], rest[n:2 * n], rest[2 * n]
    acc = x_ref[...]
    for h_ref, w_ref in zip(h_refs, w_refs):
        acc = acc + _dot(h_ref[...], w_ref[...])
    o_ref[...] = acc


def matmul_residual(x, hs, w, *, tm, casts=()):
    t, d = x.shape
    k = hs[0].shape[1]
    assert all(h.shape == (t, k) for h in hs) and w.shape == (1, k * len(hs), d) and t % tm == 0
    h_specs = [pl.BlockSpec((tm, k), lambda i: (i, 0)) for _ in hs]
    w_specs = [_resident((None, k, d), functools.partial(lambda i, p: (0, p, 0), p=p))
               for p in range(len(hs))]
    return _call_with_casts(
        _matmul_residual_body,
        grid=(t // tm,),
        in_specs=[pl.BlockSpec((tm, d), lambda i: (i, 0))] + h_specs + w_specs,
        out_spec=pl.BlockSpec((tm, d), lambda i: (i, 0)),
        out_shape=jax.ShapeDtypeStruct((t, d), F32),
        scratch_shapes=[],
        sem=("arbitrary",),
        name="out_proj",
        args=(x, *hs, *([w] * len(hs))),
        casts=casts)


def _gmlp_body(u_ref, v_ref, gain_ref, ws_ref, bs_ref, o_ref, *, chunks):
    c, gd = GMLP_CHUNK, GMLP_GROUP_DIM
    causal = _iota((c, c), 1) <= _iota((c, c), 0)
    for g in range(ws_ref.shape[0]):
        cols = slice(g * gd, (g + 1) * gd)
        v = jax.nn.gelu(v_ref[:, cols].astype(F32))
        vc = v - jnp.mean(v, axis=-1, keepdims=True)
        vn = vc * lax.rsqrt(jnp.mean(vc * vc, axis=-1, keepdims=True) + EPS) * gain_ref[g]
        vn = vn.astype(BF16)
        w = jnp.where(causal, ws_ref[g], 0.0).astype(BF16)
        bias = bs_ref[g]
        for ci in range(chunks):
            rows = slice(ci * c, (ci + 1) * c)
            mixed = _dot(w, vn[rows, :]) + bias
            u = jax.nn.gelu(u_ref[rows, cols].astype(F32))
            o_ref[rows, cols] = (u * mixed).astype(o_ref.dtype)


def gmlp(p, gain, ws, bs, *, u_blk, v_blk, tt):
    t = p.shape[0]
    groups, c, _ = ws.shape
    gd = GMLP_GROUP_DIM
    width = groups * gd
    assert t % tt == 0 and tt % c == 0
    full = lambda shape: _resident(shape, lambda i: (0,) * len(shape))
    return pl.pallas_call(
        functools.partial(_gmlp_body, chunks=tt // c),
        grid=(t // tt,),
        in_specs=[pl.BlockSpec((tt, width), lambda i: (i, u_blk)),
                  pl.BlockSpec((tt, width), lambda i: (i, v_blk)),
                  full((groups, 1, gd)), full((groups, c, c)), full((groups, c, 1))],
        out_specs=pl.BlockSpec((tt, width), lambda i: (i, 0)),
        out_shape=jax.ShapeDtypeStruct((t, width), BF16),
        compiler_params=_params("parallel"),
        name="gmlp",
    )(p, p, gain.reshape(groups, 1, gd), ws, bs.reshape(groups, c, 1))


def _compress_body(k_ref, v_ref, pe_ref, kw1_ref, kw2_ref, vw1_ref, vw2_ref, kg_ref, kc_ref, vct_ref, x_ref):
    st = CMP_STRIDE
    n = x_ref.shape[0] // st

    def mlp(src_ref, w1_ref, w2_ref):
        x_ref[...] = src_ref[...].astype(F32)

        def half(part):
            return jnp.concatenate(
                [(x_ref[pl.ds(l, n, stride=st), :] + pe_ref[part * st + l:part * st + l + 1, :]).astype(BF16)
                 for l in range(st)], axis=1)

        a = _dot(half(0), w1_ref[0])
        b = _dot(half(1), w1_ref[1])
        hid = jax.nn.gelu(a + pltpu.roll(b, n - 1, axis=0))
        return _dot(hid.astype(BF16), w2_ref[...])

    kc_ref[...] = _rms(mlp(k_ref, kw1_ref, kw2_ref), kg_ref[...]).astype(kc_ref.dtype)
    vct_ref[...] = mlp(v_ref, vw1_ref, vw2_ref).T.astype(vct_ref.dtype)


def compress(p, pe, kw1, kw2, vw1, vw2, kgain, *, batch, seq, g_kv, k_blk, v_blk):
    d = HEAD_DIM
    st = CMP_STRIDE
    n = seq // st
    hd = st * d
    assert CMP_BLOCK == 2 * st and pe.shape == (CMP_BLOCK, d) and kw1.shape == (2, hd, d)
    full = lambda shape: pl.BlockSpec(shape, lambda i, j: (0,) * len(shape))
    return pl.pallas_call(
        _compress_body,
        grid=(batch, g_kv),
        in_specs=[pl.BlockSpec((seq, d), lambda i, j: (i, k_blk + j)),
                  pl.BlockSpec((seq, d), lambda i, j: (i, v_blk + j)),
                  full((CMP_BLOCK, d)), full((2, hd, d)), full((d, d)),
                  full((2, hd, d)), full((d, d)), full((1, d))],
        out_specs=[pl.BlockSpec((None, None, n, d), lambda i, j: (i, j, 0, 0)),
                   pl.BlockSpec((None, None, d, n), lambda i, j: (i, j, 0, 0))],
        out_shape=[jax.ShapeDtypeStruct((batch, g_kv, n, d), BF16),
                   jax.ShapeDtypeStruct((batch, g_kv, d, n), BF16)],
        scratch_shapes=[pltpu.VMEM((seq, d), F32)],
        compiler_params=_params("parallel", "arbitrary"),
        name="nsa_compress",
    )(p, p, pe, kw1, kw2, vw1, vw2, kgain)


def _col_softmax(s, mask, *, may_be_empty, with_sum=True):
    s = jnp.where(mask, s, NEG)
    p = jnp.exp2(s - jnp.max(s, axis=0, keepdims=True))
    if may_be_empty:
        p = jnp.where(mask, p, 0.0)
    return (p, jnp.sum(p, axis=0, keepdims=True)) if with_sum else p


def _nsa_body(q_ref, kc_ref, vct_ref, ov_ref, kv_ref, gl_ref, qg_ref, kg_ref, o_ref,
              ksx_ref, kwn_ref, vst_ref, vwt_ref, m_ref, acc_ref, cnt_ref, *, tq, tk, n_slc, n_top):
    r_, d = NSA_GROUP, HEAD_DIM
    cols = r_ * tq
    nc = kc_ref.shape[0]
    seq = kv_ref.shape[0]
    g_kv = kv_ref.shape[1] // (4 * d)
    qi = pl.program_id(2)
    q0 = qi * tq
    head = lambda r: slice(r * tq, (r + 1) * tq)

    def stage_kv(g):
        sect = lambda n: slice((n * g_kv + g) * d, (n * g_kv + g + 1) * d)
        ksx_ref[:, 0:d] = _rms(kv_ref[:, sect(0)].astype(F32), kg_ref[1:2, :]).astype(BF16)
        ksx_ref[:, d:2 * d] = (_iota((seq, d), 0) >> SLC_SHIFT == _iota((seq, d), 1)).astype(BF16)
        kwn_ref[...] = _rms(kv_ref[:, sect(2)].astype(F32), kg_ref[2:3, :]).astype(BF16)
        for c in range(seq // tk):
            vst_ref[c, 0:d, :] = kv_ref[c * tk:(c + 1) * tk, sect(1)].astype(F32).T.astype(BF16)
            vst_ref[c, d:, :] = jnp.ones((SUM_ROWS, tk), BF16)
        for c in range(seq // tq):
            vwt_ref[c, 0:d, :] = kv_ref[c * tq:(c + 1) * tq, sect(3)].astype(F32).T.astype(BF16)
            vwt_ref[c, d:, :] = jnp.ones((SUM_ROWS, tq), BF16)

    for g in range(g_kv):
        pl.when((qi == 0) & (pl.program_id(1) == g))(functools.partial(stage_kv, g))

    qraw = q_ref[...].astype(F32)
    qt = jnp.concatenate(
        [(_rms(qraw[:, r * d:(r + 1) * d], qg_ref[...]) * (d ** -0.5 * LOG2_E)).T for r in range(r_)],
        axis=1).astype(BF16)

    sc = _dot(kc_ref[...], qt)
    mask_c = _iota((nc, tq), 0) * CMP_STRIDE + (CMP_BLOCK - 1) <= q0 + _iota((nc, tq), 1)
    o_cmp = []
    psum = None
    for r in range(r_):
        p, den = _col_softmax(sc[:, head(r)], mask_c, may_be_empty=True)
        p = p / jnp.maximum(den, 1e-30)
        psum = p if psum is None else psum + p
        o_cmp.append(_dot(vct_ref[...], p.astype(BF16)))

    overlap = ov_ref[...]
    hi = psum.astype(BF16)
    r1 = psum - hi.astype(F32)
    lo = r1.astype(BF16)
    lo2 = (r1 - lo.astype(F32)).astype(BF16)
    imp = _dot(overlap, hi) + _dot(overlap, lo) + _dot(overlap, lo2)

    j_s = _iota((n_slc, tq), 0)
    pos_s = q0 + _iota((n_slc, tq), 1)
    dlt = (pos_s >> SLC_SHIFT) - j_s
    forced = (j_s == 0) | ((dlt >= 0) & (dlt < N_LOCAL_SLC))
    val = jnp.where(dlt >= 0, imp, NEG)
    val = jnp.where(forced, FORCE_SCORE, val)
    sub = F32_SUBLANES
    groups = [val[g0:g0 + sub, :] for g0 in range(0, n_slc, sub)]
    j_g = _iota((sub, tq), 0)
    cnt_ref[...] = jnp.zeros((n_slc, tq), jnp.int32)
    last_blk = (q0 + tq - 1) >> SLC_SHIFT

    def count_beaten_by(ig):
        cnts = [jnp.zeros((sub, tq), jnp.int32) for _ in groups]
        for i in range(ig * sub, (ig + 1) * sub):
            vi = val[i:i + 1, :]
            for gi, vg in enumerate(groups):
                g0 = gi * sub
                if g0 > i:
                    beats = vi >= vg
                elif g0 + sub - 1 < i:
                    beats = vi > vg
                else:
                    beats = (vi > vg) | ((vi == vg) & (j_g + g0 > i))
                cnts[gi] = cnts[gi] + beats.astype(jnp.int32)
        cnt_ref[...] += jnp.concatenate(cnts, axis=0)

    for ig in range(n_slc // sub):
        pl.when(ig * sub <= last_blk)(functools.partial(count_beaten_by, ig))
    cnt = cnt_ref[...]
    bias = jnp.where(cnt < n_top, 0.0, NEG)
    if n_slc < d:
        bias = jnp.concatenate([bias, jnp.zeros((d - n_slc, tq), F32)], axis=0)
    qx = jnp.concatenate([qt, jnp.concatenate([bias.astype(BF16)] * r_, axis=1)], axis=0)

    m_ref[...] = jnp.full((1, cols), NEG, F32)
    acc_ref[...] = jnp.zeros((d + SUM_ROWS, cols), F32)

    def scores(kt):
        k0 = pl.multiple_of(kt * tk, tk)
        return _dot(ksx_ref[pl.ds(k0, tk), :], qx)

    def fold(kt, r, s, causal):
        if causal is not None:
            s = jnp.where(causal >= kt * tk, s, NEG)
        m_old = m_ref[:, head(r)]
        m_new = jnp.maximum(m_old, jnp.max(s, axis=0, keepdims=True))
        alpha = jnp.exp2(m_old - m_new)
        p = jnp.exp2(s - m_new)
        acc_ref[:, head(r)] = alpha * acc_ref[:, head(r)] + _dot(vst_ref[kt], p.astype(BF16))
        m_ref[:, head(r)] = m_new

    def slc_tiles(kts, causal):
        s_all = [scores(kt) for kt in kts]
        for kt, s in zip(kts, s_all):
            for r in range(r_):
                fold(kt, r, s[:, head(r)], causal)

    n_quads = q0 // (4 * tk)

    def quad(i, carry):
        slc_tiles([4 * i + dd for dd in range(4)], None)
        return carry

    lax.fori_loop(0, n_quads, quad, 0)
    causal_gap = q0 + _iota((tk, tq), 1) - _iota((tk, tq), 0)
    n_tiles = (q0 + tq + tk - 1) // tk

    def causal_pair(i, carry):
        slc_tiles([4 * n_quads + 2 * i, 4 * n_quads + 2 * i + 1], causal_gap)
        return carry

    lax.fori_loop(0, (n_tiles - 4 * n_quads + 1) // 2, causal_pair, 0)

    nw = WINDOW // tq + 1
    wc = jnp.maximum(qi - WINDOW // tq, 0)
    w0 = pl.multiple_of(wc * tq, tq)
    sw = _dot(kwn_ref[pl.ds(w0, nw * tq), :], qt)
    gap_w = q0 + _iota((nw * tq, tq), 1) - (w0 + _iota((nw * tq, tq), 0))
    mask_w = (gap_w >= 0) & (gap_w < WINDOW)
    o_win = []
    for r in range(r_):
        p = _col_softmax(sw[:, head(r)], mask_w, may_be_empty=False, with_sum=False).astype(BF16)
        o = _dot(vwt_ref[wc], p[0:tq, :])
        for c in range(1, nw):
            o = o + _dot(vwt_ref[wc + c], p[c * tq:(c + 1) * tq, :])
        o_win.append(o[0:d, :] / o[d:d + 1, :])

    gate = jax.nn.sigmoid(gl_ref[...].astype(F32)).T
    for r in range(r_):
        c = N_BRANCH * r
        o_slc = acc_ref[0:d, head(r)] / acc_ref[d:d + 1, head(r)]
        o = (gate[c:c + 1, :] * o_cmp[r] + gate[c + 1:c + 2, :] * o_slc
             + gate[c + 2:c + 3, :] * o_win[r])
        o_ref[:, r * d:(r + 1) * d] = o.T.astype(o_ref.dtype)


def nsa_attention(p, kc, vct, q_gain, k_gain, *, batch, seq, col, tq, tk, casts=()):
    b, g, nc, d = kc.shape
    r_ = NSA_GROUP
    nq = seq // tq
    n_slc = seq // SLC_BLOCK
    assert seq % tq == 0 and seq % tk == 0 and WINDOW % tq == 0 and WINDOW + tq <= seq
    assert ((tq == tk and nq % 2 == 0) or tq == 2 * tk) and n_slc <= d
    cols = r_ * tq
    cmp_start = jnp.arange(nc)[None, :] * CMP_STRIDE
    slc_start = jnp.arange(n_slc)[:, None] * SLC_BLOCK
    overlap = ((cmp_start < slc_start + SLC_BLOCK) & (cmp_start + CMP_BLOCK > slc_start)).astype(BF16)
    kv4 = 4 * g * d
    return _call_with_casts(
        functools.partial(_nsa_body, tq=tq, tk=tk, n_slc=n_slc, n_top=min(N_SLC, n_slc)),
        grid=(batch, g, nq),
        in_specs=[pl.BlockSpec((tq, r_ * d), lambda bi, gi, qi: (bi * nq + qi, col["q"] // r_ + gi)),
                  pl.BlockSpec((None, None, nc, d), lambda bi, gi, qi: (bi, gi, 0, 0)),
                  pl.BlockSpec((None, None, d, nc), lambda bi, gi, qi: (bi, gi, 0, 0)),
                  _resident((n_slc, nc), lambda bi, gi, qi: (0, 0)),
                  pl.BlockSpec((seq, kv4), lambda bi, gi, qi: (bi, col["kv4"])),
                  pl.BlockSpec((tq, LANES), lambda bi, gi, qi: (bi * nq + qi, col["gate"] + gi)),
                  pl.BlockSpec((1, d), lambda bi, gi, qi: (0, 0)),
                  pl.BlockSpec((N_BRANCH, d), lambda bi, gi, qi: (0, 0))],
        out_spec=pl.BlockSpec((tq, r_ * d), lambda bi, gi, qi: (bi * nq + qi, gi)),
        out_shape=jax.ShapeDtypeStruct((batch * seq, g * r_ * d), BF16),
        scratch_shapes=[pltpu.VMEM((seq, 2 * d), BF16), pltpu.VMEM((seq, d), BF16),
                        pltpu.VMEM((seq // tk, d + SUM_ROWS, tk), BF16),
                        pltpu.VMEM((seq // tq, d + SUM_ROWS, tq), BF16),
                        pltpu.VMEM((1, cols), F32), pltpu.VMEM((d + SUM_ROWS, cols), F32),
                        pltpu.VMEM((n_slc, tq), jnp.int32)],
        sem=("arbitrary", "arbitrary", "arbitrary"),
        name="nsa_attention",
        args=(p, kc, vct, overlap, p, p, q_gain.reshape(1, d), k_gain),
        casts=casts)


def _even_mixer(x, batch, seq, norm_g, w_in, q_gain, k_gain, cmp_pe, kw1, kw2, vw1, vw2,
                gmlp_norm, gmlp_ws, gmlp_b, casts, nsa_casts):
    d = HEAD_DIM
    dm = x.shape[1]
    gmlp_w = gmlp_ws.shape[0] * GMLP_GROUP_DIM
    qw = dm - gmlp_w
    g_kv = max(1, qw // d // NSA_GROUP)
    kv = g_kv * d
    n_kv = 2 * N_BRANCH
    n_gate = N_BRANCH * qw // d
    assert w_in.shape[1] == qw + n_kv * kv + n_gate + 2 * gmlp_w
    o_gl = qw + n_kv * kv
    o_u = o_gl + n_gate
    per_g = n_gate // g_kv
    pad_cols = lambda w, n: jnp.pad(w, ((0, 0), (0, n - w.shape[1])))
    gate_cols = [pad_cols(w_in[:, o_gl + gi * per_g: o_gl + (gi + 1) * per_g], LANES) for gi in range(g_kv)]
    w_nsa = jnp.concatenate([w_in[:, :qw], w_in[:, qw + 2 * kv:o_gl], w_in[:, qw:qw + 2 * kv]] + gate_cols, axis=1)
    u_blk = -(-w_nsa.shape[1] // gmlp_w)
    w_cat = jnp.concatenate([pad_cols(w_nsa, u_blk * gmlp_w), w_in[:, o_u:]], axis=1).astype(BF16)
    tn = w_cat.shape[1]
    assert qw % (4 * kv) == 0
    col = {"q": 0, "kv4": qw // (4 * kv), "kc": (qw + 4 * kv) // d, "vc": (qw + 5 * kv) // d, "gate": o_gl // d}

    p, cast_w = norm_matmul(x, norm_g, w_cat, tm=512, tn=tn, rc=256, casts=casts)

    hd = CMP_STRIDE * d
    kc, vct = compress(p, cmp_pe, kw1.reshape(2, hd, d).astype(BF16), kw2.astype(BF16),
                       vw1.reshape(2, hd, d).astype(BF16), vw2.astype(BF16), k_gain[0:1],
                       batch=batch, seq=seq, g_kv=g_kv, k_blk=col["kc"], v_blk=col["vc"])
    oa, nsa_cast_w = nsa_attention(p, kc, vct, q_gain, k_gain, batch=batch, seq=seq, col=col, tq=512, tk=256,
                                   casts=nsa_casts)
    ob = gmlp(p, gmlp_norm, gmlp_ws, gmlp_b, u_blk=u_blk, v_blk=u_blk + 1, tt=512)
    return oa, ob, cast_w, nsa_cast_w


def kernel(x, ev_norm, ev_w_in, ev_q_gain, ev_k_gain, ev_cmp_pe, ev_cmp_k_w1, ev_cmp_k_w2, ev_cmp_v_w1, ev_cmp_v_w2, ev_gmlp_norm, ev_gmlp_ws, ev_gmlp_b, ev_w_out, od_norm, od_w_in, od_conv_w, od_w_out, ffn_norm, ffn_w_in, ffn_conv_w, ffn_conv_b, ffn_w_down):
    batch, seq, dm = x.shape
    depth = ffn_norm.shape[0]
    no_bias = jnp.zeros((od_conv_w.shape[0], od_conv_w.shape[2]), F32)
    in_tiles = dict(seq=seq, tm=1024, tn=512, rc=256)
    sconv_tiles = dict(seq=seq, tm=512, tn=od_w_in.shape[2] // 3, rc=256)
    h = x.reshape(batch * seq, dm)
    ffn_in_w = od_in_w = None
    for i in range(depth):
        j = i // 2
        if i % 2 == 0:
            oa, ob, (out_w,), cast_w = _even_mixer(
                h, batch, seq, ev_norm[j], ev_w_in[j], ev_q_gain[j], ev_k_gain[j], ev_cmp_pe[j],
                ev_cmp_k_w1[j], ev_cmp_k_w2[j], ev_cmp_v_w1[j], ev_cmp_v_w2[j], ev_gmlp_norm[j],
                ev_gmlp_ws[j], ev_gmlp_b[j], [(ev_w_out, j)], [(ffn_w_in, i)] if ffn_in_w is None else [])
            ffn_in_w = cast_w[0] if cast_w else ffn_in_w
            h, _ = matmul_residual(h, [oa, ob], out_w, tm=512)
        else:
            m, (out_w,) = gated_in_proj(h, od_norm, od_in_w, 0, od_conv_w, no_bias, layer=j, mode="sconv",
                                        casts=[(od_w_out, j)], **sconv_tiles)
            h, _ = matmul_residual(h, [m], out_w, tm=512)
        next_odd = i + 1 < depth and (i + 1) % 2 == 1
        casts = [(ffn_w_down, i)] + ([(od_w_in, (i + 1) // 2)] if next_odd else [])
        f, cast_w = gated_in_proj(h, ffn_norm, ffn_in_w, 0, ffn_conv_w, ffn_conv_b, layer=i, mode="ffn",
                                  casts=casts, **in_tiles)
        od_in_w = cast_w[1] if next_odd else None
        h, cast_next = matmul_residual(h, [f], cast_w[0], tm=256,
                                       casts=[(ffn_w_in, i + 1)] if i + 1 < depth else [])
        ffn_in_w = cast_next[0] if cast_next else None
    return h.reshape(batch, seq, dm)
```
